```python
import jax, jax.numpy as jnp
from jax import lax
import numpy as np

D_MODEL = 1024
BATCH = 8
SEQ = 4096
DEPTH = 4

GRID_W = 64
BLOCK_Q = 128
CHUNK = 128
ROPE_THETA = 10000.0
EPS = 1e-6
A_HEADS = 8
A_KV_HEADS = 2
A_HEAD_DIM = 64
A_WIDTH = A_HEADS * A_HEAD_DIM
A_KV_WIDTH = A_KV_HEADS * A_HEAD_DIM
B_GROUPS = 4
B_GROUP_DIM = 128
B_WIDTH = B_GROUPS * B_GROUP_DIM
M_HEADS = 4
M_HEAD_DIM = 128
M_WIDTH = M_HEADS * M_HEAD_DIM
MEM_LEN = 256
N_BRANCH = 3
BRANCH_WIDTH = 512
IN_SPLITS = (A_WIDTH, A_KV_WIDTH, A_KV_WIDTH, A_WIDTH,
             B_WIDTH, B_WIDTH, B_WIDTH,
             M_WIDTH, M_WIDTH,
             N_BRANCH * D_MODEL)
IN_WIDTH = sum(IN_SPLITS)

kernel_name = "hybrid_gqa_gmlp_memory_encoder"


def _split_points():
    pts, acc = [], 0
    for s in IN_SPLITS[:-1]:
        acc += s
        pts.append(acc)
    return pts


def rms_norm(x, g):
    xf = x.astype(jnp.float32)
    y = xf * lax.rsqrt(jnp.mean(xf * xf, axis=-1, keepdims=True) + EPS)
    return (y * g.astype(jnp.float32)).astype(x.dtype)


def layer_norm(x, g, b):
    xf = x.astype(jnp.float32)
    mu = jnp.mean(xf, axis=-1, keepdims=True)
    xc = xf - mu
    y = xc * lax.rsqrt(jnp.mean(xc * xc, axis=-1, keepdims=True) + EPS)
    return (y * g.astype(jnp.float32) + b.astype(jnp.float32)).astype(x.dtype)


def axial_rope_tables(seq):
    rows = seq // GRID_W
    row = jnp.repeat(jnp.arange(rows, dtype=jnp.float32), GRID_W)
    col = jnp.tile(jnp.arange(GRID_W, dtype=jnp.float32), rows)
    n_freq = A_HEAD_DIM // 4
    inv = ROPE_THETA ** (-jnp.arange(n_freq, dtype=jnp.float32) / n_freq)
    ang = jnp.stack([row[:, None] * inv, col[:, None] * inv], axis=1)
    return jnp.cos(ang), jnp.sin(ang)


def apply_axial_rope(x, cos, sin):
    b, s, h, d = x.shape
    x5 = x.reshape(b, s, h, 2, 2, d // 4)
    x1, x2 = x5[..., 0, :], x5[..., 1, :]
    c = cos[None, :, None].astype(x.dtype)
    sn = sin[None, :, None].astype(x.dtype)
    out = jnp.stack([x1 * c - x2 * sn, x2 * c + x1 * sn], axis=-2)
    return out.reshape(b, s, h, d)


def gqa_axial_attention(q, k, v, q_g, k_g, cos, sin):
    bsz, seq, _ = q.shape
    grp = A_HEADS // A_KV_HEADS
    q = apply_axial_rope(rms_norm(q.reshape(bsz, seq, A_HEADS, A_HEAD_DIM), q_g), cos, sin)
    k = apply_axial_rope(rms_norm(k.reshape(bsz, seq, A_KV_HEADS, A_HEAD_DIM), k_g), cos, sin)
    v = v.reshape(bsz, seq, A_KV_HEADS, A_HEAD_DIM)
    scale = A_HEAD_DIM ** -0.5
    nblk = seq // BLOCK_Q
    qb = q.reshape(bsz, nblk, BLOCK_Q, A_KV_HEADS, grp, A_HEAD_DIM).transpose(1, 0, 3, 4, 2, 5)
    kt = k.transpose(0, 2, 1, 3)
    vt = v.transpose(0, 2, 1, 3)

    def one_block(qblk):
        s = jnp.einsum('bkgqd,bksd->bkgqs', qblk, kt,
                       preferred_element_type=jnp.float32) * scale
        p = jax.nn.softmax(s, axis=-1)
        return jnp.einsum('bkgqs,bksd->bkgqd', p.astype(vt.dtype), vt)

    o = lax.map(one_block, qb)
    return o.transpose(1, 0, 4, 2, 3, 5).reshape(bsz, seq, A_WIDTH)


def chunked_spatial_gating(u, v, ln_g, ln_b, w_s, b_s):
    bsz, seq, _ = v.shape
    v = layer_norm(v, ln_g, ln_b)
    vc = v.reshape(bsz, seq // CHUNK, CHUNK, B_GROUPS, B_GROUP_DIM)
    mixed = jnp.einsum('gpq,bnqgc->bnpgc', w_s, vc) + b_s.T[None, None, :, :, None]
    return u * mixed.reshape(bsz, seq, B_WIDTH)


def memory_cross_attention(q, mem_n, w_kv):
    bsz, seq, _ = q.shape
    kv = mem_n @ w_kv
    k, v = jnp.split(kv, 2, axis=-1)
    k = k.reshape(bsz, -1, M_HEADS, M_HEAD_DIM)
    v = v.reshape(bsz, -1, M_HEADS, M_HEAD_DIM)
    q = q.reshape(bsz, seq, M_HEADS, M_HEAD_DIM)
    s = jnp.einsum('bshd,bmhd->bhsm', q, k,
                   preferred_element_type=jnp.float32) * (M_HEAD_DIM ** -0.5)
    p = jax.nn.softmax(s, axis=-1)
    o = jnp.einsum('bhsm,bmhd->bshd', p.astype(v.dtype), v)
    return o.reshape(bsz, seq, M_WIDTH)


def setup_inputs(seed: int = 0) -> dict:
    key = jax.random.key(seed)
    ks = jax.random.split(key, 16)
    nrm = jax.random.normal
    f32 = jnp.float32
    return {
        "x": nrm(ks[0], (BATCH, SEQ, D_MODEL), f32),
        "mem": nrm(ks[1], (BATCH, MEM_LEN, D_MODEL), f32),
        "norm_g": 1.0 + 0.01 * nrm(ks[2], (DEPTH, D_MODEL), f32),
        "w_in": nrm(ks[3], (DEPTH, D_MODEL, IN_WIDTH), f32) * D_MODEL ** -0.5,
        "q_norm_g": 1.0 + 0.01 * nrm(ks[4], (DEPTH, A_HEAD_DIM), f32),
        "k_norm_g": 1.0 + 0.01 * nrm(ks[5], (DEPTH, A_HEAD_DIM), f32),
        "sg_ln_g": 1.0 + 0.01 * nrm(ks[6], (DEPTH, B_WIDTH), f32),
        "sg_ln_b": 0.01 * nrm(ks[7], (DEPTH, B_WIDTH), f32),
        "w_s": nrm(ks[8], (DEPTH, B_GROUPS, CHUNK, CHUNK), f32) * CHUNK ** -0.5,
        "b_s": 1.0 + 0.01 * nrm(ks[9], (DEPTH, B_GROUPS, CHUNK), f32),
        "mem_norm_g": 1.0 + 0.01 * nrm(ks[10], (DEPTH, D_MODEL), f32),
        "w_mem_kv": nrm(ks[11], (DEPTH, D_MODEL, 2 * M_WIDTH), f32) * D_MODEL ** -0.5,
        "w_br": nrm(ks[12], (DEPTH, N_BRANCH, BRANCH_WIDTH, D_MODEL), f32) * BRANCH_WIDTH ** -0.5,
        "w_out": nrm(ks[13], (DEPTH, D_MODEL, D_MODEL), f32) * D_MODEL ** -0.5,
        "final_g": 1.0 + 0.01 * nrm(ks[14], (D_MODEL,), f32),
    }


def reference(x, mem, norm_g, w_in, q_norm_g, k_norm_g, sg_ln_g, sg_ln_b, w_s, b_s,
              mem_norm_g, w_mem_kv, w_br, w_out, final_g):
    bsz, seq, d = x.shape
    cos, sin = axial_rope_tables(seq)
    pts = _split_points()
    for l in range(DEPTH):
        h = rms_norm(x, norm_g[l])
        proj = h @ w_in[l]
        qA, kA, vA, zA, uB, vB, zB, qM, zM, g_logits = jnp.split(proj, pts, axis=-1)
        yA = gqa_axial_attention(qA, kA, vA, q_norm_g[l], k_norm_g[l], cos, sin) * jax.nn.silu(zA)
        yB = chunked_spatial_gating(uB, vB, sg_ln_g[l], sg_ln_b[l], w_s[l], b_s[l]) * jax.nn.silu(zB)
        mem_n = rms_norm(mem, mem_norm_g[l])
        yM = memory_cross_attention(qM, mem_n, w_mem_kv[l]) * jax.nn.silu(zM)
        branches = jnp.stack([yA, yB, yM], axis=2)
        up = jnp.einsum('bsnw,nwd->bsnd', branches, w_br[l])
        gates = jax.nn.sigmoid(g_logits.reshape(bsz, seq, N_BRANCH, d))
        merged = jnp.sum(gates * up, axis=2)
        x = x + merged @ w_out[l]
    return rms_norm(x, final_g)
```

```python
import functools
import math

import jax
import jax.numpy as jnp
from jax import lax
from jax.experimental import pallas as pl
from jax.experimental.pallas import tpu as pltpu

D_MODEL = 1024
DEPTH = 4
GRID_W = 64
CHUNK = 128
ROPE_THETA = 10000.0
EPS = 1e-6
A_HEADS = 8
A_KV_HEADS = 2
A_HEAD_DIM = 64
A_WIDTH = A_HEADS * A_HEAD_DIM
A_KV_WIDTH = A_KV_HEADS * A_HEAD_DIM
B_GROUPS = 4
B_GROUP_DIM = 128
B_WIDTH = B_GROUPS * B_GROUP_DIM
M_HEADS = 4
M_HEAD_DIM = 128
M_WIDTH = M_HEADS * M_HEAD_DIM
MEM_LEN = 256
N_BRANCH = 3
QKV_WIDTH = A_WIDTH + 2 * A_KV_WIDTH
REST_WIDTH = 512 + 3 * 512 + 2 * 512 + N_BRANCH * D_MODEL

LANES = 128
LOG2E = math.log2(math.e)
VMEM_LIMIT_BYTES = 56 * 1024 * 1024

QKV_TILE = 512
ATTN_TILE = 512
REST_TILE = 256

F32 = jnp.float32
BF16 = jnp.bfloat16


def _rms(x, g):
    return x * lax.rsqrt(jnp.mean(x * x, axis=-1, keepdims=True) + EPS) * g


def _dot(a, b):
    return jnp.dot(a, b, preferred_element_type=F32)


def _const_spec(shape):
    nd = len(shape)
    return pl.BlockSpec(shape, lambda *_: (0,) * nd, pipeline_mode=pl.Buffered(1))


def _mem_kernel(mem_ref, g_ref, w_ref, k_ref, v_ref):
    n = _rms(mem_ref[...], g_ref[...]).astype(BF16)
    kv = _dot(n, w_ref[...])
    k_ref[...] = (kv[:, :M_WIDTH] * (M_HEAD_DIM ** -0.5 * LOG2E)).astype(BF16)
    v_ref[...] = kv[:, M_WIDTH:].astype(BF16)


def _mem_kv(mem, mem_norm_g, w_mem_kv):
    bsz = mem.shape[0]
    out = jax.ShapeDtypeStruct((DEPTH, bsz, MEM_LEN, M_WIDTH), BF16)
    return pl.pallas_call(
        _mem_kernel,
        grid=(DEPTH, bsz),
        in_specs=[
            pl.BlockSpec((None, MEM_LEN, D_MODEL), lambda l, b: (b, 0, 0)),
            pl.BlockSpec((None, 1, D_MODEL), lambda l, b: (l, 0, 0)),
            pl.BlockSpec((None, D_MODEL, 2 * M_WIDTH), lambda l, b: (l, 0, 0)),
        ],
        out_specs=[
            pl.BlockSpec((None, None, MEM_LEN, M_WIDTH), lambda l, b: (l, b, 0, 0)),
            pl.BlockSpec((None, None, MEM_LEN, M_WIDTH), lambda l, b: (l, b, 0, 0)),
        ],
        out_shape=[out, out],
        compiler_params=pltpu.CompilerParams(
            dimension_semantics=("arbitrary", "arbitrary"),
            vmem_limit_bytes=VMEM_LIMIT_BYTES),
        name="mem_kv",
    )(mem, mem_norm_g.reshape(DEPTH, 1, D_MODEL), w_mem_kv)


def _qkv_kernel(x_ref, g_ref, w_ref, gq_ref, gk_ref, cos_ref, sin_ref, ones_ref,
                qT_ref, k_ref, vT_ref):
    tm = x_ref.shape[0]
    h = _rms(x_ref[...], g_ref[...]).astype(BF16)
    qkv = _dot(h, w_ref[...])
    cos = cos_ref[...]
    sin = sin_ref[...]
    ones = ones_ref[...]
    lane = lax.broadcasted_iota(jnp.int32, (tm, LANES), 1)
    first_half = (lane & 16) == 0

    def norm_rope(t, gvec):
        sq = t * t
        hi = sq.astype(BF16)
        lo = (sq - hi.astype(F32)).astype(BF16)
        ms = (_dot(hi, ones) + _dot(lo, ones)) * (1.0 / A_HEAD_DIM)
        tn = t * lax.rsqrt(ms + EPS) * gvec
        partner = jnp.where(first_half,
                            pltpu.roll(tn, LANES - 16, 1),
                            pltpu.roll(tn, 16, 1))
        return tn * cos + partner * sin

    zeros = jnp.zeros((A_HEAD_DIM, tm), BF16)
    gq = gq_ref[...]
    for i in range(A_WIDTH // LANES):
        qt = norm_rope(qkv[:, i * LANES:(i + 1) * LANES], gq).T.astype(BF16)
        kv_head = (2 * i) // (A_HEADS // A_KV_HEADS)
        for half in range(2):
            base = (2 * i + half) * LANES
            piece = qt[half * A_HEAD_DIM:(half + 1) * A_HEAD_DIM, :]
            lo_rows = pl.ds(base, A_HEAD_DIM)
            hi_rows = pl.ds(base + A_HEAD_DIM, A_HEAD_DIM)
            if kv_head == 0:
                qT_ref[lo_rows, :] = piece
                qT_ref[hi_rows, :] = zeros
            else:
                qT_ref[lo_rows, :] = zeros
                qT_ref[hi_rows, :] = piece
    k_ref[...] = norm_rope(qkv[:, A_WIDTH:A_WIDTH + A_KV_WIDTH], gk_ref[...]).astype(BF16)
    vT_ref[...] = qkv[:, A_WIDTH + A_KV_WIDTH:].T.astype(BF16)


def _qkv(x, g, w_qkv, gq, gk, cos, sin, ones):
    bsz, seq, _ = x.shape
    tm = QKV_TILE
    return pl.pallas_call(
        _qkv_kernel,
        grid=(bsz, seq // tm),
        in_specs=[
            pl.BlockSpec((None, tm, D_MODEL), lambda b, i: (b, i, 0)),
            _const_spec((1, D_MODEL)),
            _const_spec((D_MODEL, QKV_WIDTH)),
            _const_spec((1, LANES)),
            _const_spec((1, LANES)),
            pl.BlockSpec((tm, LANES), lambda b, i: (i, 0)),
            pl.BlockSpec((tm, LANES), lambda b, i: (i, 0)),
            _const_spec((LANES, LANES)),
        ],
        out_specs=[
            pl.BlockSpec((None, A_HEADS * LANES, tm), lambda b, i: (b, 0, i)),
            pl.BlockSpec((None, tm, A_KV_WIDTH), lambda b, i: (b, i, 0)),
            pl.BlockSpec((None, A_KV_WIDTH, tm), lambda b, i: (b, 0, i)),
        ],
        out_shape=[
            jax.ShapeDtypeStruct((bsz, A_HEADS * LANES, seq), BF16),
            jax.ShapeDtypeStruct((bsz, seq, A_KV_WIDTH), BF16),
            jax.ShapeDtypeStruct((bsz, A_KV_WIDTH, seq), BF16),
        ],
        compiler_params=pltpu.CompilerParams(
            dimension_semantics=("arbitrary", "arbitrary"),
            vmem_limit_bytes=VMEM_LIMIT_BYTES),
        name="qkv",
    )(x, g, w_qkv, gq, gk, cos, sin, ones)


def _attn_kernel(q_ref, k_ref, vT_ref, o_ref):
    s = _dot(k_ref[...], q_ref[...])
    m = jnp.max(s, axis=0, keepdims=True)
    p = jnp.exp2(s - m)
    l = jnp.sum(p, axis=0, keepdims=True)
    pv = _dot(vT_ref[...], p.astype(BF16))
    o_ref[...] = pv * (1.0 / l)


def _attn(qT, k, vT):
    bsz, seq, _ = k.shape
    tq = ATTN_TILE
    grp = A_HEADS // A_KV_HEADS
    return pl.pallas_call(
        _attn_kernel,
        grid=(bsz, seq // tq, A_HEADS),
        in_specs=[
            pl.BlockSpec((None, LANES, tq), lambda b, i, h: (b, h, i)),
            pl.BlockSpec((None, seq, A_KV_WIDTH), lambda b, i, h: (b, 0, 0)),
            pl.BlockSpec((None, A_HEAD_DIM, seq), lambda b, i, h: (b, h // grp, 0)),
        ],
        out_specs=pl.BlockSpec((None, A_HEAD_DIM, tq), lambda b, i, h: (b, h, i)),
        out_shape=jax.ShapeDtypeStruct((bsz, A_WIDTH, seq), F32),
        compiler_params=pltpu.CompilerParams(
            dimension_semantics=("arbitrary", "arbitrary", "arbitrary"),
            vmem_limit_bytes=VMEM_LIMIT_BYTES),
        name="attn",
    )(qT, k, vT)


def _rest_kernel(x_ref, yAT_ref, kmem_ref, vmem_ref, g_ref, w_ref, lng_ref, lnb_ref,
                 ws_ref, bs_ref, wbr_ref, wout_ref, fg_ref, o_ref, *, final):
    tq = x_ref.shape[0]
    x = x_ref[...]
    h = _rms(x, g_ref[...]).astype(BF16)
    proj = _dot(h, w_ref[...])
    zA = proj[:, 0:512]
    uB = proj[:, 512:1024]
    vB = proj[:, 1024:1536]
    zB = proj[:, 1536:2048]
    qM = proj[:, 2048:2560]
    zM = proj[:, 2560:3072]

    yA = yAT_ref[...].T * jax.nn.silu(zA)

    mu = jnp.mean(vB, axis=-1, keepdims=True)
    vc = vB - mu
    vn = vc * lax.rsqrt(jnp.mean(vc * vc, axis=-1, keepdims=True) + EPS)
    vn = (vn * lng_ref[...] + lnb_ref[...]).astype(BF16)
    rows = []
    for c in range(tq // CHUNK):
        cols = []
        for g in range(B_GROUPS):
            blk = vn[c * CHUNK:(c + 1) * CHUNK, g * B_GROUP_DIM:(g + 1) * B_GROUP_DIM]
            cols.append(_dot(ws_ref[g], blk) + bs_ref[g])
        rows.append(jnp.concatenate(cols, axis=1))
    mixed = jnp.concatenate(rows, axis=0)
    yB = uB * mixed * jax.nn.silu(zB)

    heads = []
    for hh in range(M_HEADS):
        sl = slice(hh * M_HEAD_DIM, (hh + 1) * M_HEAD_DIM)
        s = lax.dot_general(qM[:, sl].astype(BF16), kmem_ref[:, sl],
                            (((1,), (1,)), ((), ())), preferred_element_type=F32)
        p = jnp.exp2(s - jnp.max(s, axis=-1, keepdims=True))
        l = jnp.sum(p, axis=-1, keepdims=True)
        heads.append(_dot(p.astype(BF16), vmem_ref[:, sl]) * (1.0 / l))
    yM = jnp.concatenate(heads, axis=1) * jax.nn.silu(zM)

    merged = None
    for n, y in enumerate((yA, yB, yM)):
        up = _dot(y.astype(BF16), wbr_ref[n])
        gate = jax.nn.sigmoid(proj[:, 3072 + n * D_MODEL:3072 + (n + 1) * D_MODEL])
        merged = gate * up if merged is None else merged + gate * up
    out = x + _dot(merged.astype(BF16), wout_ref[...])
    if final:
        out = _rms(out, fg_ref[...])
    o_ref[...] = out


def _rest(x, yAT, kmem, vmem, layer, g, w_rest, lng, lnb, ws, bs, wbr, wout, fg, final):
    bsz, seq, _ = x.shape
    tq = REST_TILE
    return pl.pallas_call(
        functools.partial(_rest_kernel, final=final),
        grid=(bsz, seq // tq),
        in_specs=[
            pl.BlockSpec((None, tq, D_MODEL), lambda b, i: (b, i, 0)),
            pl.BlockSpec((None, A_WIDTH, tq), lambda b, i: (b, 0, i)),
            pl.BlockSpec((None, None, MEM_LEN, M_WIDTH), lambda b, i: (layer, b, 0, 0)),
            pl.BlockSpec((None, None, MEM_LEN, M_WIDTH), lambda b, i: (layer, b, 0, 0)),
            _const_spec((1, D_MODEL)),
            _const_spec((D_MODEL, REST_WIDTH)),
            _const_spec((1, B_WIDTH)),
            _const_spec((1, B_WIDTH)),
            _const_spec((B_GROUPS, CHUNK, CHUNK)),
            _const_spec((B_GROUPS, CHUNK, B_GROUP_DIM)),
            _const_spec((N_BRANCH, 512, D_MODEL)),
            _const_spec((D_MODEL, D_MODEL)),
            _const_spec((1, D_MODEL)),
        ],
        out_specs=pl.BlockSpec((None, tq, D_MODEL), lambda b, i: (b, i, 0)),
        out_shape=jax.ShapeDtypeStruct((bsz, seq, D_MODEL), F32),
        compiler_params=pltpu.CompilerParams(
            dimension_semantics=("arbitrary", "arbitrary"),
            vmem_limit_bytes=VMEM_LIMIT_BYTES),
        name="rest",
    )(x, yAT, kmem, vmem, g, w_rest, lng, lnb, ws, bs, wbr, wout, fg)


def _rope_tables(seq):
    rows = seq // GRID_W
    row = jnp.repeat(jnp.arange(rows, dtype=F32), GRID_W)
    col = jnp.tile(jnp.arange(GRID_W, dtype=F32), rows)
    n_freq = A_HEAD_DIM // 4
    inv = ROPE_THETA ** (-jnp.arange(n_freq, dtype=F32) / n_freq)
    ang = jnp.stack([row[:, None] * inv, col[:, None] * inv], axis=1)
    cos, sin = jnp.cos(ang), jnp.sin(ang)
    cos_hd = jnp.broadcast_to(cos[:, :, None, :], (seq, 2, 2, n_freq)).reshape(seq, A_HEAD_DIM)
    sin_hd = jnp.stack([-sin, sin], axis=2).reshape(seq, A_HEAD_DIM)
    reps = LANES // A_HEAD_DIM
    return jnp.tile(cos_hd, (1, reps)), jnp.tile(sin_hd, (1, reps))


def kernel(x, mem, norm_g, w_in, q_norm_g, k_norm_g, sg_ln_g, sg_ln_b, w_s, b_s,
           mem_norm_g, w_mem_kv, w_br, w_out, final_g):
    bsz, seq, _ = x.shape
    cos, sin = _rope_tables(seq)
    lane_head = jnp.arange(LANES) // A_HEAD_DIM
    ones = (lane_head[:, None] == lane_head[None, :]).astype(BF16)
    reps = LANES // A_HEAD_DIM
    q_scale = A_HEAD_DIM ** -0.5 * LOG2E

    w_in_bf = w_in.astype(BF16)
    kmem, vmem = _mem_kv(mem, mem_norm_g, w_mem_kv.astype(BF16))
    fg = final_g.reshape(1, D_MODEL)
    for l in range(DEPTH):
        g = norm_g[l].reshape(1, D_MODEL)
        gq = (jnp.tile(q_norm_g[l], reps) * q_scale).reshape(1, LANES)
        gk = jnp.tile(k_norm_g[l], reps).reshape(1, LANES)
        qT, k, vT = _qkv(x, g, w_in_bf[l, :, :QKV_WIDTH], gq, gk, cos, sin, ones)
        yAT = _attn(qT, k, vT)
        bs = jnp.broadcast_to(b_s[l][:, :, None], (B_GROUPS, CHUNK, B_GROUP_DIM))
        x = _rest(x, yAT, kmem, vmem, l, g, w_in_bf[l, :, QKV_WIDTH:],
                  sg_ln_g[l].reshape(1, B_WIDTH), sg_ln_b[l].reshape(1, B_WIDTH),
                  w_s[l].astype(BF16), bs, w_br[l].astype(BF16), w_out[l].astype(BF16),
                  fg, final=(l == DEPTH - 1))
    return x
```

```python
import functools
import math

import jax
import jax.numpy as jnp
from jax import lax
from jax.experimental import pallas as pl
from jax.experimental.pallas import tpu as pltpu

D_MODEL = 1024
DEPTH = 4
GRID_W = 64
CHUNK = 128
ROPE_THETA = 10000.0
EPS = 1e-6
A_HEADS = 8
A_KV_HEADS = 2
A_HEAD_DIM = 64
A_WIDTH = A_HEADS * A_HEAD_DIM
A_KV_WIDTH = A_KV_HEADS * A_HEAD_DIM
B_GROUPS = 4
B_GROUP_DIM = 128
B_WIDTH = B_GROUPS * B_GROUP_DIM
M_HEADS = 4
M_HEAD_DIM = 128
M_WIDTH = M_HEADS * M_HEAD_DIM
MEM_LEN = 256
N_BRANCH = 3
QKV_WIDTH = A_WIDTH + 2 * A_KV_WIDTH
REST_WIDTH = 512 + 3 * 512 + 2 * 512 + N_BRANCH * D_MODEL

LANES = 128
LOG2E = math.log2(math.e)
VMEM_LIMIT_BYTES = 56 * 1024 * 1024

QKV_TILE = 512
ATTN_TILE = 512
ATTN_KV_CHUNK = 512
REST_TILE = 256

F32 = jnp.float32
BF16 = jnp.bfloat16


def _rms(x, g):
    return x * lax.rsqrt(jnp.mean(x * x, axis=-1, keepdims=True) + EPS) * g


def _dot(a, b):
    return jnp.dot(a, b, preferred_element_type=F32)


def _const_spec(shape):
    nd = len(shape)
    return pl.BlockSpec(shape, lambda *_: (0,) * nd, pipeline_mode=pl.Buffered(1))


def _mem_kernel(mem_ref, g_ref, w_ref, k_ref, v_ref):
    n = _rms(mem_ref[...], g_ref[...]).astype(BF16)
    kv = _dot(n, w_ref[...])
    k_ref[...] = (kv[:, :M_WIDTH] * (M_HEAD_DIM ** -0.5 * LOG2E)).astype(BF16)
    v_ref[...] = kv[:, M_WIDTH:].astype(BF16)


def _mem_kv(mem, mem_norm_g, w_mem_kv):
    bsz = mem.shape[0]
    out = jax.ShapeDtypeStruct((DEPTH, bsz, MEM_LEN, M_WIDTH), BF16)
    return pl.pallas_call(
        _mem_kernel,
        grid=(DEPTH, bsz),
        in_specs=[
            pl.BlockSpec((None, MEM_LEN, D_MODEL), lambda l, b: (b, 0, 0)),
            pl.BlockSpec((None, 1, D_MODEL), lambda l, b: (l, 0, 0)),
            pl.BlockSpec((None, D_MODEL, 2 * M_WIDTH), lambda l, b: (l, 0, 0)),
        ],
        out_specs=[
            pl.BlockSpec((None, None, MEM_LEN, M_WIDTH), lambda l, b: (l, b, 0, 0)),
            pl.BlockSpec((None, None, MEM_LEN, M_WIDTH), lambda l, b: (l, b, 0, 0)),
        ],
        out_shape=[out, out],
        compiler_params=pltpu.CompilerParams(
            dimension_semantics=("arbitrary", "arbitrary"),
            vmem_limit_bytes=VMEM_LIMIT_BYTES),
        name="mem_kv",
    )(mem, mem_norm_g.reshape(DEPTH, 1, D_MODEL), w_mem_kv)


def _qkv_kernel(x_ref, g_ref, w_ref, gq_ref, gk_ref, cos_ref, sin_ref, ones_ref,
                qT_ref, k_ref, vT_ref):
    tm = x_ref.shape[0]
    h = _rms(x_ref[...], g_ref[...]).astype(BF16)
    qkv = _dot(h, w_ref[...])
    cos = cos_ref[...]
    sin = sin_ref[...]
    ones = ones_ref[...]
    lane = lax.broadcasted_iota(jnp.int32, (tm, LANES), 1)
    first_half = (lane & 16) == 0

    def norm_rope(t, gvec):
        sq = t * t
        hi = sq.astype(BF16)
        lo = (sq - hi.astype(F32)).astype(BF16)
        ms = (_dot(hi, ones) + _dot(lo, ones)) * (1.0 / A_HEAD_DIM)
        tn = t * lax.rsqrt(ms + EPS) * gvec
        partner = jnp.where(first_half,
                            pltpu.roll(tn, LANES - 16, 1),
                            pltpu.roll(tn, 16, 1))
        return tn * cos + partner * sin

    zeros = jnp.zeros((A_HEAD_DIM, tm), BF16)
    gq = gq_ref[...]
    for i in range(A_WIDTH // LANES):
        qt = norm_rope(qkv[:, i * LANES:(i + 1) * LANES], gq).T.astype(BF16)
        kv_head = (2 * i) // (A_HEADS // A_KV_HEADS)
        for half in range(2):
            base = (2 * i + half) * LANES
            piece = qt[half * A_HEAD_DIM:(half + 1) * A_HEAD_DIM, :]
            lo_rows = pl.ds(base, A_HEAD_DIM)
            hi_rows = pl.ds(base + A_HEAD_DIM, A_HEAD_DIM)
            if kv_head == 0:
                qT_ref[lo_rows, :] = piece
                qT_ref[hi_rows, :] = zeros
            else:
                qT_ref[lo_rows, :] = zeros
                qT_ref[hi_rows, :] = piece
    k_ref[...] = norm_rope(qkv[:, A_WIDTH:A_WIDTH + A_KV_WIDTH], gk_ref[...]).astype(BF16)
    vT_ref[...] = qkv[:, A_WIDTH + A_KV_WIDTH:].T.astype(BF16)


def _qkv(x, g, w_qkv, gq, gk, cos, sin, ones):
    bsz, seq, _ = x.shape
    tm = QKV_TILE
    return pl.pallas_call(
        _qkv_kernel,
        grid=(bsz, seq // tm),
        in_specs=[
            pl.BlockSpec((None, tm, D_MODEL), lambda b, i: (b, i, 0)),
            _const_spec((1, D_MODEL)),
            _const_spec((D_MODEL, QKV_WIDTH)),
            _const_spec((1, LANES)),
            _const_spec((1, LANES)),
            pl.BlockSpec((tm, LANES), lambda b, i: (i, 0)),
            pl.BlockSpec((tm, LANES), lambda b, i: (i, 0)),
            _const_spec((LANES, LANES)),
        ],
        out_specs=[
            pl.BlockSpec((None, A_HEADS * LANES, tm), lambda b, i: (b, 0, i)),
            pl.BlockSpec((None, tm, A_KV_WIDTH), lambda b, i: (b, i, 0)),
            pl.BlockSpec((None, A_KV_WIDTH, tm), lambda b, i: (b, 0, i)),
        ],
        out_shape=[
            jax.ShapeDtypeStruct((bsz, A_HEADS * LANES, seq), BF16),
            jax.ShapeDtypeStruct((bsz, seq, A_KV_WIDTH), BF16),
            jax.ShapeDtypeStruct((bsz, A_KV_WIDTH, seq), BF16),
        ],
        compiler_params=pltpu.CompilerParams(
            dimension_semantics=("arbitrary", "arbitrary"),
            vmem_limit_bytes=VMEM_LIMIT_BYTES),
        name="qkv",
    )(x, g, w_qkv, gq, gk, cos, sin, ones)


def _attn_kernel(q_ref, k_ref, vT_ref, o_ref, s_even, s_odd):
    seq, tq = s_even.shape
    n_chunks = seq // ATTN_KV_CHUNK
    grp = A_HEADS // A_KV_HEADS

    def rows8(a):
        return a.reshape(a.shape[0] // 8, 8, tq)

    def stage(t_qk, s_qk, t_sm, s_sm, m_sm):
        if t_qk is not None:
            q = q_ref[t_qk]
            m8 = None
        if t_sm is not None:
            v_rows = pl.ds(pl.multiple_of((t_sm // grp) * A_HEAD_DIM, A_HEAD_DIM), A_HEAD_DIM)
            acc = jnp.zeros((A_HEAD_DIM, tq), F32)
            l8 = jnp.zeros((8, tq), F32)
        for c in range(n_chunks):
            kv = slice(c * ATTN_KV_CHUNK, (c + 1) * ATTN_KV_CHUNK)
            if t_qk is not None:
                sc = _dot(k_ref[kv, :], q)
                s_qk[kv, :] = sc
                cm = jnp.max(rows8(sc), axis=0)
                m8 = cm if m8 is None else jnp.maximum(m8, cm)
            if t_sm is not None:
                p = jnp.exp2(s_sm[kv, :] - m_sm)
                l8 = l8 + jnp.sum(rows8(p), axis=0)
                acc = acc + _dot(vT_ref[v_rows, kv], p.astype(BF16))
        if t_sm is not None:
            l = jnp.sum(l8, axis=0, keepdims=True)
            o_rows = pl.ds(pl.multiple_of(t_sm * A_HEAD_DIM, A_HEAD_DIM), A_HEAD_DIM)
            o_ref[o_rows, :] = acc * (1.0 / l)
        if t_qk is not None:
            return jnp.max(m8, axis=0, keepdims=True)
        return None

    m_first = stage(0, s_even, None, None, None)

    def pair(u, m_even):
        m_odd = stage(2 * u + 1, s_odd, 2 * u, s_even, m_even)
        return stage(2 * u + 2, s_even, 2 * u + 1, s_odd, m_odd)

    m_even = lax.fori_loop(0, A_HEADS // 2 - 1, pair, m_first)
    m_odd = stage(A_HEADS - 1, s_odd, A_HEADS - 2, s_even, m_even)
    stage(None, None, A_HEADS - 1, s_odd, m_odd)


def _attn(qT, k, vT):
    bsz, seq, _ = k.shape
    tq = ATTN_TILE
    return pl.pallas_call(
        _attn_kernel,
        grid=(bsz, seq // tq),
        in_specs=[
            pl.BlockSpec((None, A_HEADS, LANES, tq), lambda b, i: (b, 0, 0, i)),
            pl.BlockSpec((None, seq, A_KV_WIDTH), lambda b, i: (b, 0, 0)),
            pl.BlockSpec((None, A_KV_WIDTH, seq), lambda b, i: (b, 0, 0)),
        ],
        out_specs=pl.BlockSpec((None, A_WIDTH, tq), lambda b, i: (b, 0, i)),
        out_shape=jax.ShapeDtypeStruct((bsz, A_WIDTH, seq), F32),
        scratch_shapes=[pltpu.VMEM((seq, tq), F32), pltpu.VMEM((seq, tq), F32)],
        compiler_params=pltpu.CompilerParams(
            dimension_semantics=("arbitrary", "arbitrary"),
            vmem_limit_bytes=VMEM_LIMIT_BYTES),
        name="attn",
    )(qT.reshape(bsz, A_HEADS, LANES, seq), k, vT)


def _rest_kernel(x_ref, yAT_ref, kmem_ref, vmem_ref, g_ref, w_ref, lng_ref, lnb_ref,
                 ws_ref, bs_ref, wbr_ref, wout_ref, fg_ref, o_ref, *, final):
    tq = x_ref.shape[0]
    x = x_ref[...]
    h = _rms(x, g_ref[...]).astype(BF16)
    proj = _dot(h, w_ref[...])
    zA = proj[:, 0:512]
    uB = proj[:, 512:1024]
    vB = proj[:, 1024:1536]
    zB = proj[:, 1536:2048]
    qM = proj[:, 2048:2560]
    zM = proj[:, 2560:3072]

    yA = yAT_ref[...].T * jax.nn.silu(zA)

    mu = jnp.mean(vB, axis=-1, keepdims=True)
    vc = vB - mu
    vn = vc * lax.rsqrt(jnp.mean(vc * vc, axis=-1, keepdims=True) + EPS)
    vn = (vn * lng_ref[...] + lnb_ref[...]).astype(BF16)
    rows = []
    for c in range(tq // CHUNK):
        cols = []
        for g in range(B_GROUPS):
            blk = vn[c * CHUNK:(c + 1) * CHUNK, g * B_GROUP_DIM:(g + 1) * B_GROUP_DIM]
            cols.append(_dot(ws_ref[g], blk) + bs_ref[g])
        rows.append(jnp.concatenate(cols, axis=1))
    mixed = jnp.concatenate(rows, axis=0)
    yB = uB * mixed * jax.nn.silu(zB)

    heads = []
    for hh in range(M_HEADS):
        sl = slice(hh * M_HEAD_DIM, (hh + 1) * M_HEAD_DIM)
        s = lax.dot_general(qM[:, sl].astype(BF16), kmem_ref[:, sl],
                            (((1,), (1,)), ((), ())), preferred_element_type=F32)
        p = jnp.exp2(s - jnp.max(s, axis=-1, keepdims=True))
        l = jnp.sum(p, axis=-1, keepdims=True)
        heads.append(_dot(p.astype(BF16), vmem_ref[:, sl]) * (1.0 / l))
    yM = jnp.concatenate(heads, axis=1) * jax.nn.silu(zM)

    merged = None
    for n, y in enumerate((yA, yB, yM)):
        up = _dot(y.astype(BF16), wbr_ref[n])
        gate = jax.nn.sigmoid(proj[:, 3072 + n * D_MODEL:3072 + (n + 1) * D_MODEL])
        merged = gate * up if merged is None else merged + gate * up
    out = x + _dot(merged.astype(BF16), wout_ref[...])
    if final:
        out = _rms(out, fg_ref[...])
    o_ref[...] = out


def _rest(x, yAT, kmem, vmem, layer, g, w_rest, lng, lnb, ws, bs, wbr, wout, fg, final):
    bsz, seq, _ = x.shape
    tq = REST_TILE
    return pl.pallas_call(
        functools.partial(_rest_kernel, final=final),
        grid=(bsz, seq // tq),
        in_specs=[
            pl.BlockSpec((None, tq, D_MODEL), lambda b, i: (b, i, 0)),
            pl.BlockSpec((None, A_WIDTH, tq), lambda b, i: (b, 0, i)),
            pl.BlockSpec((None, None, MEM_LEN, M_WIDTH), lambda b, i: (layer, b, 0, 0)),
            pl.BlockSpec((None, None, MEM_LEN, M_WIDTH), lambda b, i: (layer, b, 0, 0)),
            _const_spec((1, D_MODEL)),
            _const_spec((D_MODEL, REST_WIDTH)),
            _const_spec((1, B_WIDTH)),
            _const_spec((1, B_WIDTH)),
            _const_spec((B_GROUPS, CHUNK, CHUNK)),
            _const_spec((B_GROUPS, CHUNK, B_GROUP_DIM)),
            _const_spec((N_BRANCH, 512, D_MODEL)),
            _const_spec((D_MODEL, D_MODEL)),
            _const_spec((1, D_MODEL)),
        ],
        out_specs=pl.BlockSpec((None, tq, D_MODEL), lambda b, i: (b, i, 0)),
        out_shape=jax.ShapeDtypeStruct((bsz, seq, D_MODEL), F32),
        compiler_params=pltpu.CompilerParams(
            dimension_semantics=("arbitrary", "arbitrary"),
            vmem_limit_bytes=VMEM_LIMIT_BYTES),
        name="rest",
    )(x, yAT, kmem, vmem, g, w_rest, lng, lnb, ws, bs, wbr, wout, fg)


def _rope_tables(seq):
    rows = seq // GRID_W
    row = jnp.repeat(jnp.arange(rows, dtype=F32), GRID_W)
    col = jnp.tile(jnp.arange(GRID_W, dtype=F32), rows)
    n_freq = A_HEAD_DIM // 4
    inv = ROPE_THETA ** (-jnp.arange(n_freq, dtype=F32) / n_freq)
    ang = jnp.stack([row[:, None] * inv, col[:, None] * inv], axis=1)
    cos, sin = jnp.cos(ang), jnp.sin(ang)
    cos_hd = jnp.broadcast_to(cos[:, :, None, :], (seq, 2, 2, n_freq)).reshape(seq, A_HEAD_DIM)
    sin_hd = jnp.stack([-sin, sin], axis=2).reshape(seq, A_HEAD_DIM)
    reps = LANES // A_HEAD_DIM
    return jnp.tile(cos_hd, (1, reps)), jnp.tile(sin_hd, (1, reps))


def kernel(x, mem, norm_g, w_in, q_norm_g, k_norm_g, sg_ln_g, sg_ln_b, w_s, b_s,
           mem_norm_g, w_mem_kv, w_br, w_out, final_g):
    bsz, seq, _ = x.shape
    cos, sin = _rope_tables(seq)
    lane_head = jnp.arange(LANES) // A_HEAD_DIM
    ones = (lane_head[:, None] == lane_head[None, :]).astype(BF16)
    reps = LANES // A_HEAD_DIM
    q_scale = A_HEAD_DIM ** -0.5 * LOG2E

    w_in_bf = w_in.astype(BF16)
    kmem, vmem = _mem_kv(mem, mem_norm_g, w_mem_kv.astype(BF16))
    fg = final_g.reshape(1, D_MODEL)
    for l in range(DEPTH):
        g = norm_g[l].reshape(1, D_MODEL)
        gq = (jnp.tile(q_norm_g[l], reps) * q_scale).reshape(1, LANES)
        gk = jnp.tile(k_norm_g[l], reps).reshape(1, LANES)
        qT, k, vT = _qkv(x, g, w_in_bf[l, :, :QKV_WIDTH], gq, gk, cos, sin, ones)
        yAT = _attn(qT, k, vT)
        bs = jnp.broadcast_to(b_s[l][:, :, None], (B_GROUPS, CHUNK, B_GROUP_DIM))
        x = _rest(x, yAT, kmem, vmem, l, g, w_in_bf[l, :, QKV_WIDTH:],
                  sg_ln_g[l].reshape(1, B_WIDTH), sg_ln_b[l].reshape(1, B_WIDTH),
                  w_s[l].astype(BF16), bs, w_br[l].astype(BF16), w_out[l].astype(BF16),
                  fg, final=(l == DEPTH - 1))
    return x
```

```python
import functools
import math

import jax
import jax.numpy as jnp
from jax import lax
from jax.experimental import pallas as pl
from jax.experimental.pallas import tpu as pltpu

D_MODEL = 1024
DEPTH = 4
GRID_W = 64
CHUNK = 128
ROPE_THETA = 10000.0
EPS = 1e-6
A_HEADS = 8
A_KV_HEADS = 2
A_HEAD_DIM = 64
A_WIDTH = A_HEADS * A_HEAD_DIM
A_KV_WIDTH = A_KV_HEADS * A_HEAD_DIM
B_GROUPS = 4
B_GROUP_DIM = 128
B_WIDTH = B_GROUPS * B_GROUP_DIM
M_HEADS = 4
M_HEAD_DIM = 128
M_WIDTH = M_HEADS * M_HEAD_DIM
MEM_LEN = 256
N_BRANCH = 3
QKV_WIDTH = A_WIDTH + 2 * A_KV_WIDTH
REST_WIDTH = 512 + 3 * 512 + 2 * 512 + N_BRANCH * D_MODEL

LANES = 128
LOG2E = math.log2(math.e)
VMEM_LIMIT_BYTES = 56 * 1024 * 1024

QKV_TILE = 512
ATTN_TILE = 512
ATTN_KV_CHUNK = 256
REST_TILE = 512

F32 = jnp.float32
BF16 = jnp.bfloat16


def _rms(x, g):
    return x * lax.rsqrt(jnp.mean(x * x, axis=-1, keepdims=True) + EPS) * g


def _dot(a, b):
    return jnp.dot(a, b, preferred_element_type=F32)


def _const_spec(shape):
    nd = len(shape)
    return pl.BlockSpec(shape, lambda *_: (0,) * nd, pipeline_mode=pl.Buffered(1))


def _layer_spec(shape, layer):
    nd = len(shape)
    return pl.BlockSpec((None,) + shape, lambda *_: (layer,) + (0,) * nd,
                        pipeline_mode=pl.Buffered(1))


def _mem_kernel(mem_ref, g_ref, w_ref, k_ref, v_ref):
    n = _rms(mem_ref[...], g_ref[...]).astype(BF16)
    kv = _dot(n, w_ref[...])
    k_ref[...] = (kv[:, :M_WIDTH] * (M_HEAD_DIM ** -0.5 * LOG2E)).astype(BF16)
    v_ref[...] = kv[:, M_WIDTH:].astype(BF16)


def _mem_kv(mem, mem_norm_g, w_mem_kv):
    bsz = mem.shape[0]
    out = jax.ShapeDtypeStruct((DEPTH, bsz, MEM_LEN, M_WIDTH), BF16)
    return pl.pallas_call(
        _mem_kernel,
        grid=(DEPTH, bsz),
        in_specs=[
            pl.BlockSpec((None, MEM_LEN, D_MODEL), lambda l, b: (b, 0, 0)),
            pl.BlockSpec((None, 1, D_MODEL), lambda l, b: (l, 0, 0)),
            pl.BlockSpec((None, D_MODEL, 2 * M_WIDTH), lambda l, b: (l, 0, 0)),
        ],
        out_specs=[
            pl.BlockSpec((None, None, MEM_LEN, M_WIDTH), lambda l, b: (l, b, 0, 0)),
            pl.BlockSpec((None, None, MEM_LEN, M_WIDTH), lambda l, b: (l, b, 0, 0)),
        ],
        out_shape=[out, out],
        compiler_params=pltpu.CompilerParams(
            dimension_semantics=("arbitrary", "arbitrary"),
            vmem_limit_bytes=VMEM_LIMIT_BYTES),
        name="mem_kv",
    )(mem, mem_norm_g.reshape(DEPTH, 1, D_MODEL), w_mem_kv)


def _qkv_kernel(x_ref, g_ref, w_ref, gq_ref, gk_ref, cos_ref, sin_ref, ones_ref,
                qT_ref, k_ref, vT_ref):
    tm = x_ref.shape[0]
    h = _rms(x_ref[...], g_ref[...]).astype(BF16)
    qkv = _dot(h, w_ref[...])
    cos = cos_ref[...]
    sin = sin_ref[...]
    ones = ones_ref[...]
    lane = lax.broadcasted_iota(jnp.int32, (tm, LANES), 1)
    first_half = (lane & 16) == 0

    def norm_rope(t, gvec):
        sq = t * t
        hi = sq.astype(BF16)
        lo = (sq - hi.astype(F32)).astype(BF16)
        ms = (_dot(hi, ones) + _dot(lo, ones)) * (1.0 / A_HEAD_DIM)
        tn = t * lax.rsqrt(ms + EPS) * gvec
        partner = jnp.where(first_half,
                            pltpu.roll(tn, LANES - 16, 1),
                            pltpu.roll(tn, 16, 1))
        return tn * cos + partner * sin

    zeros = jnp.zeros((A_HEAD_DIM, tm), BF16)
    gq = gq_ref[...]
    for i in range(A_WIDTH // LANES):
        qt = norm_rope(qkv[:, i * LANES:(i + 1) * LANES], gq).T.astype(BF16)
        kv_head = (2 * i) // (A_HEADS // A_KV_HEADS)
        for half in range(2):
            base = (2 * i + half) * LANES
            piece = qt[half * A_HEAD_DIM:(half + 1) * A_HEAD_DIM, :]
            lo_rows = pl.ds(base, A_HEAD_DIM)
            hi_rows = pl.ds(base + A_HEAD_DIM, A_HEAD_DIM)
            if kv_head == 0:
                qT_ref[lo_rows, :] = piece
                qT_ref[hi_rows, :] = zeros
            else:
                qT_ref[lo_rows, :] = zeros
                qT_ref[hi_rows, :] = piece
    k_ref[...] = norm_rope(qkv[:, A_WIDTH:A_WIDTH + A_KV_WIDTH], gk_ref[...]).astype(BF16)
    vT_ref[...] = qkv[:, A_WIDTH + A_KV_WIDTH:].T.astype(BF16)


def _qkv(x, layer, g, w_qkv, gq, gk, cos, sin, ones):
    bsz, seq, _ = x.shape
    tm = QKV_TILE
    return pl.pallas_call(
        _qkv_kernel,
        grid=(bsz, seq // tm),
        in_specs=[
            pl.BlockSpec((None, tm, D_MODEL), lambda b, i: (b, i, 0)),
            _layer_spec((1, D_MODEL), layer),
            _layer_spec((D_MODEL, QKV_WIDTH), layer),
            _layer_spec((1, LANES), layer),
            _layer_spec((1, LANES), layer),
            pl.BlockSpec((tm, LANES), lambda b, i: (i, 0)),
            pl.BlockSpec((tm, LANES), lambda b, i: (i, 0)),
            _const_spec((LANES, LANES)),
        ],
        out_specs=[
            pl.BlockSpec((None, A_HEADS * LANES, tm), lambda b, i: (b, 0, i)),
            pl.BlockSpec((None, tm, A_KV_WIDTH), lambda b, i: (b, i, 0)),
            pl.BlockSpec((None, A_KV_WIDTH, tm), lambda b, i: (b, 0, i)),
        ],
        out_shape=[
            jax.ShapeDtypeStruct((bsz, A_HEADS * LANES, seq), BF16),
            jax.ShapeDtypeStruct((bsz, seq, A_KV_WIDTH), BF16),
            jax.ShapeDtypeStruct((bsz, A_KV_WIDTH, seq), BF16),
        ],
        compiler_params=pltpu.CompilerParams(
            dimension_semantics=("arbitrary", "arbitrary"),
            vmem_limit_bytes=VMEM_LIMIT_BYTES),
        name="qkv",
    )(x, g, w_qkv, gq, gk, cos, sin, ones)


def _attn_kernel(q_ref, k_ref, vT_ref, o_ref, s_even, s_odd):
    seq, tq = s_even.shape
    n_chunks = seq // ATTN_KV_CHUNK
    grp = A_HEADS // A_KV_HEADS

    def rows8(a):
        return a.reshape(a.shape[0] // 8, 8, tq)

    def stage(t_qk, s_qk, t_sm, s_sm, m_sm):
        if t_qk is not None:
            q = q_ref[t_qk]
            m8 = None
        if t_sm is not None:
            v_rows = pl.ds(pl.multiple_of((t_sm // grp) * A_HEAD_DIM, A_HEAD_DIM), A_HEAD_DIM)
            acc = jnp.zeros((A_HEAD_DIM, tq), F32)
            l8 = jnp.zeros((8, tq), F32)
        for c in range(n_chunks):
            kv = slice(c * ATTN_KV_CHUNK, (c + 1) * ATTN_KV_CHUNK)
            if t_qk is not None:
                sc = _dot(k_ref[kv, :], q)
                s_qk[kv, :] = sc
                cm = jnp.max(rows8(sc), axis=0)
                m8 = cm if m8 is None else jnp.maximum(m8, cm)
            if t_sm is not None:
                p = jnp.exp2(s_sm[kv, :] - m_sm)
                l8 = l8 + jnp.sum(rows8(p), axis=0)
                acc = acc + _dot(vT_ref[v_rows, kv], p.astype(BF16))
        if t_sm is not None:
            l = jnp.sum(l8, axis=0, keepdims=True)
            o_rows = pl.ds(pl.multiple_of(t_sm * A_HEAD_DIM, A_HEAD_DIM), A_HEAD_DIM)
            o_ref[o_rows, :] = acc * (1.0 / l)
        if t_qk is not None:
            return jnp.max(m8, axis=0, keepdims=True)
        return None

    m_first = stage(0, s_even, None, None, None)

    def pair(u, m_even):
        m_odd = stage(2 * u + 1, s_odd, 2 * u, s_even, m_even)
        return stage(2 * u + 2, s_even, 2 * u + 1, s_odd, m_odd)

    m_even = lax.fori_loop(0, A_HEADS // 2 - 1, pair, m_first)
    m_odd = stage(A_HEADS - 1, s_odd, A_HEADS - 2, s_even, m_even)
    stage(None, None, A_HEADS - 1, s_odd, m_odd)


def _attn(qT, k, vT):
    bsz, seq, _ = k.shape
    tq = ATTN_TILE
    return pl.pallas_call(
        _attn_kernel,
        grid=(bsz, seq // tq),
        in_specs=[
            pl.BlockSpec((None, A_HEADS, LANES, tq), lambda b, i: (b, 0, 0, i)),
            pl.BlockSpec((None, seq, A_KV_WIDTH), lambda b, i: (b, 0, 0)),
            pl.BlockSpec((None, A_KV_WIDTH, seq), lambda b, i: (b, 0, 0)),
        ],
        out_specs=pl.BlockSpec((None, A_WIDTH, tq), lambda b, i: (b, 0, i)),
        out_shape=jax.ShapeDtypeStruct((bsz, A_WIDTH, seq), F32),
        scratch_shapes=[pltpu.VMEM((seq, tq), F32), pltpu.VMEM((seq, tq), F32)],
        compiler_params=pltpu.CompilerParams(
            dimension_semantics=("arbitrary", "arbitrary"),
            vmem_limit_bytes=VMEM_LIMIT_BYTES),
        name="attn",
    )(qT.reshape(bsz, A_HEADS, LANES, seq), k, vT)


def _rest_kernel(x_ref, yAT_ref, kmem_ref, vmem_ref, g_ref, w_ref, lng_ref, lnb_ref,
                 ws_ref, bs_ref, wbr_ref, wout_ref, fg_ref, o_ref, *, final):
    tq = x_ref.shape[0]
    x = x_ref[...]
    h = _rms(x, g_ref[...]).astype(BF16)
    proj = _dot(h, w_ref[...])
    zA = proj[:, 0:512]
    uB = proj[:, 512:1024]
    vB = proj[:, 1024:1536]
    zB = proj[:, 1536:2048]
    qM = proj[:, 2048:2560]
    zM = proj[:, 2560:3072]

    yA = yAT_ref[...].T * jax.nn.silu(zA)

    mu = jnp.mean(vB, axis=-1, keepdims=True)
    vc = vB - mu
    vn = vc * lax.rsqrt(jnp.mean(vc * vc, axis=-1, keepdims=True) + EPS)
    vn = (vn * lng_ref[...] + lnb_ref[...]).astype(BF16)
    rows = []
    for c in range(tq // CHUNK):
        cols = []
        for g in range(B_GROUPS):
            blk = vn[c * CHUNK:(c + 1) * CHUNK, g * B_GROUP_DIM:(g + 1) * B_GROUP_DIM]
            cols.append(_dot(ws_ref[g], blk) + bs_ref[g])
        rows.append(jnp.concatenate(cols, axis=1))
    mixed = jnp.concatenate(rows, axis=0)
    yB = uB * mixed * jax.nn.silu(zB)

    heads = []
    for hh in range(M_HEADS):
        sl = slice(hh * M_HEAD_DIM, (hh + 1) * M_HEAD_DIM)
        s = lax.dot_general(qM[:, sl].astype(BF16), kmem_ref[:, sl],
                            (((1,), (1,)), ((), ())), preferred_element_type=F32)
        p = jnp.exp2(s - jnp.max(s, axis=-1, keepdims=True))
        l = jnp.sum(p, axis=-1, keepdims=True)
        heads.append(_dot(p.astype(BF16), vmem_ref[:, sl]) * (1.0 / l))
    yM = jnp.concatenate(heads, axis=1) * jax.nn.silu(zM)

    merged = None
    for n, y in enumerate((yA, yB, yM)):
        up = _dot(y.astype(BF16), wbr_ref[n])
        gate = jax.nn.sigmoid(proj[:, 3072 + n * D_MODEL:3072 + (n + 1) * D_MODEL])
        merged = gate * up if merged is None else merged + gate * up
    out = x + _dot(merged.astype(BF16), wout_ref[...])
    if final:
        out = _rms(out, fg_ref[...])
    o_ref[...] = out


def _rest(x, yAT, kmem, vmem, layer, g, w_rest, lng, lnb, ws, bs, wbr, wout, fg, final):
    bsz, seq, _ = x.shape
    tq = REST_TILE
    return pl.pallas_call(
        functools.partial(_rest_kernel, final=final),
        grid=(bsz, seq // tq),
        in_specs=[
            pl.BlockSpec((None, tq, D_MODEL), lambda b, i: (b, i, 0)),
            pl.BlockSpec((None, A_WIDTH, tq), lambda b, i: (b, 0, i)),
            pl.BlockSpec((None, None, MEM_LEN, M_WIDTH), lambda b, i: (layer, b, 0, 0)),
            pl.BlockSpec((None, None, MEM_LEN, M_WIDTH), lambda b, i: (layer, b, 0, 0)),
            _layer_spec((1, D_MODEL), layer),
            _layer_spec((D_MODEL, REST_WIDTH), layer),
            _layer_spec((1, B_WIDTH), layer),
            _layer_spec((1, B_WIDTH), layer),
            _layer_spec((B_GROUPS, CHUNK, CHUNK), layer),
            _layer_spec((B_GROUPS, CHUNK, B_GROUP_DIM), layer),
            _layer_spec((N_BRANCH, 512, D_MODEL), layer),
            _layer_spec((D_MODEL, D_MODEL), layer),
            _const_spec((1, D_MODEL)),
        ],
        out_specs=pl.BlockSpec((None, tq, D_MODEL), lambda b, i: (b, i, 0)),
        out_shape=jax.ShapeDtypeStruct((bsz, seq, D_MODEL), F32),
        compiler_params=pltpu.CompilerParams(
            dimension_semantics=("arbitrary", "arbitrary"),
            vmem_limit_bytes=VMEM_LIMIT_BYTES),
        name="rest",
    )(x, yAT, kmem, vmem, g, w_rest, lng, lnb, ws, bs, wbr, wout, fg)


def _rope_tables(seq):
    rows = seq // GRID_W
    row = jnp.repeat(jnp.arange(rows, dtype=F32), GRID_W)
    col = jnp.tile(jnp.arange(GRID_W, dtype=F32), rows)
    n_freq = A_HEAD_DIM // 4
    inv = ROPE_THETA ** (-jnp.arange(n_freq, dtype=F32) / n_freq)
    ang = jnp.stack([row[:, None] * inv, col[:, None] * inv], axis=1)
    cos, sin = jnp.cos(ang), jnp.sin(ang)
    cos_hd = jnp.broadcast_to(cos[:, :, None, :], (seq, 2, 2, n_freq)).reshape(seq, A_HEAD_DIM)
    sin_hd = jnp.stack([-sin, sin], axis=2).reshape(seq, A_HEAD_DIM)
    reps = LANES // A_HEAD_DIM
    return jnp.tile(cos_hd, (1, reps)), jnp.tile(sin_hd, (1, reps))


def kernel(x, mem, norm_g, w_in, q_norm_g, k_norm_g, sg_ln_g, sg_ln_b, w_s, b_s,
           mem_norm_g, w_mem_kv, w_br, w_out, final_g):
    bsz, seq, _ = x.shape
    cos, sin = _rope_tables(seq)
    lane_head = jnp.arange(LANES) // A_HEAD_DIM
    ones = (lane_head[:, None] == lane_head[None, :]).astype(BF16)
    reps = LANES // A_HEAD_DIM
    q_scale = A_HEAD_DIM ** -0.5 * LOG2E

    w_qkv = w_in[:, :, :QKV_WIDTH].astype(BF16)
    w_rest = w_in[:, :, QKV_WIDTH:].astype(BF16)
    g = norm_g.reshape(DEPTH, 1, D_MODEL)
    gq = (jnp.tile(q_norm_g, (1, reps)) * q_scale).reshape(DEPTH, 1, LANES)
    gk = jnp.tile(k_norm_g, (1, reps)).reshape(DEPTH, 1, LANES)
    lng = sg_ln_g.reshape(DEPTH, 1, B_WIDTH)
    lnb = sg_ln_b.reshape(DEPTH, 1, B_WIDTH)
    ws = w_s.astype(BF16)
    bs = jnp.broadcast_to(b_s[:, :, :, None], (DEPTH, B_GROUPS, CHUNK, B_GROUP_DIM))
    wbr = w_br.astype(BF16)
    wout = w_out.astype(BF16)
    fg = final_g.reshape(1, D_MODEL)
    kmem, vmem = _mem_kv(mem, mem_norm_g, w_mem_kv.astype(BF16))
    for l in range(DEPTH):
        qT, k, vT = _qkv(x, l, g, w_qkv, gq, gk, cos, sin, ones)
        yAT = _attn(qT, k, vT)
        x = _rest(x, yAT, kmem, vmem, l, g, w_rest, lng, lnb, ws, bs, wbr, wout, fg,
                  final=(l == DEPTH - 1))
    return x
```

```python
import functools
import math

import jax
import jax.numpy as jnp
from jax import lax
from jax.experimental import pallas as pl
from jax.experimental.pallas import tpu as pltpu

D_MODEL = 1024
DEPTH = 4
GRID_W = 64
CHUNK = 128
ROPE_THETA = 10000.0
EPS = 1e-6
A_HEADS = 8
A_KV_HEADS = 2
A_HEAD_DIM = 64
A_WIDTH = A_HEADS * A_HEAD_DIM
A_KV_WIDTH = A_KV_HEADS * A_HEAD_DIM
B_GROUPS = 4
B_GROUP_DIM = 128
B_WIDTH = B_GROUPS * B_GROUP_DIM
M_HEADS = 4
M_HEAD_DIM = 128
M_WIDTH = M_HEADS * M_HEAD_DIM
MEM_LEN = 256
N_BRANCH = 3
QKV_WIDTH = A_WIDTH + 2 * A_KV_WIDTH
REST_WIDTH = 512 + 3 * 512 + 2 * 512 + N_BRANCH * D_MODEL

LANES = 128
LOG2E = math.log2(math.e)
VMEM_LIMIT_BYTES = 56 * 1024 * 1024

QKV_TILE = 512
ATTN_TILE = 512
ATTN_KV_CHUNK = 256
REST_TILE = 512

F32 = jnp.float32
BF16 = jnp.bfloat16


def _rms(x, g):
    return x * lax.rsqrt(jnp.mean(x * x, axis=-1, keepdims=True) + EPS) * g


def _dot(a, b):
    return jnp.dot(a, b, preferred_element_type=F32)


def _const_spec(shape):
    nd = len(shape)
    return pl.BlockSpec(shape, lambda *_: (0,) * nd, pipeline_mode=pl.Buffered(1))


def _layer_spec(shape, layer):
    nd = len(shape)
    return pl.BlockSpec((None,) + shape, lambda *_: (layer,) + (0,) * nd,
                        pipeline_mode=pl.Buffered(1))


def _mem_kernel(mem_ref, g_ref, w_ref, k_ref, v_ref):
    n = _rms(mem_ref[...], g_ref[...]).astype(BF16)
    kv = _dot(n, w_ref[...])
    k_ref[...] = (kv[:, :M_WIDTH] * (M_HEAD_DIM ** -0.5 * LOG2E)).astype(BF16)
    v_ref[...] = kv[:, M_WIDTH:].astype(BF16)


def _mem_kv(mem, mem_norm_g, w_mem_kv):
    bsz = mem.shape[0]
    out = jax.ShapeDtypeStruct((DEPTH, bsz, MEM_LEN, M_WIDTH), BF16)
    return pl.pallas_call(
        _mem_kernel,
        grid=(DEPTH, bsz),
        in_specs=[
            pl.BlockSpec((None, MEM_LEN, D_MODEL), lambda l, b: (b, 0, 0)),
            pl.BlockSpec((None, 1, D_MODEL), lambda l, b: (l, 0, 0)),
            pl.BlockSpec((None, D_MODEL, 2 * M_WIDTH), lambda l, b: (l, 0, 0)),
        ],
        out_specs=[
            pl.BlockSpec((None, None, MEM_LEN, M_WIDTH), lambda l, b: (l, b, 0, 0)),
            pl.BlockSpec((None, None, MEM_LEN, M_WIDTH), lambda l, b: (l, b, 0, 0)),
        ],
        out_shape=[out, out],
        compiler_params=pltpu.CompilerParams(
            dimension_semantics=("arbitrary", "arbitrary"),
            vmem_limit_bytes=VMEM_LIMIT_BYTES),
        name="mem_kv",
    )(mem, mem_norm_g.reshape(DEPTH, 1, D_MODEL), w_mem_kv)


def _qkv_body(x, g_ref, w_ref, gq_ref, gk_ref, cos_ref, sin_ref, ones_ref,
              qT_ref, k_ref, vT_ref):
    tm = x.shape[0]
    h = _rms(x, g_ref[...]).astype(BF16)
    qkv = _dot(h, w_ref[...])
    cos = cos_ref[...]
    sin = sin_ref[...]
    ones = ones_ref[...]
    lane = lax.broadcasted_iota(jnp.int32, (tm, LANES), 1)
    first_half = (lane & 16) == 0

    def norm_rope(t, gvec):
        sq = t * t
        hi = sq.astype(BF16)
        lo = (sq - hi.astype(F32)).astype(BF16)
        ms = (_dot(hi, ones) + _dot(lo, ones)) * (1.0 / A_HEAD_DIM)
        tn = t * lax.rsqrt(ms + EPS) * gvec
        partner = jnp.where(first_half,
                            pltpu.roll(tn, LANES - 16, 1),
                            pltpu.roll(tn, 16, 1))
        return tn * cos + partner * sin

    zeros = jnp.zeros((A_HEAD_DIM, tm), BF16)
    gq = gq_ref[...]
    for i in range(A_WIDTH // LANES):
        qt = norm_rope(qkv[:, i * LANES:(i + 1) * LANES], gq).T.astype(BF16)
        kv_head = (2 * i) // (A_HEADS // A_KV_HEADS)
        for half in range(2):
            base = (2 * i + half) * LANES
            piece = qt[half * A_HEAD_DIM:(half + 1) * A_HEAD_DIM, :]
            lo_rows = pl.ds(base, A_HEAD_DIM)
            hi_rows = pl.ds(base + A_HEAD_DIM, A_HEAD_DIM)
            if kv_head == 0:
                qT_ref[lo_rows, :] = piece
                qT_ref[hi_rows, :] = zeros
            else:
                qT_ref[lo_rows, :] = zeros
                qT_ref[hi_rows, :] = piece
    k_ref[...] = norm_rope(qkv[:, A_WIDTH:A_WIDTH + A_KV_WIDTH], gk_ref[...]).astype(BF16)
    vT_ref[...] = qkv[:, A_WIDTH + A_KV_WIDTH:].T.astype(BF16)


N_QKV_PARAMS = 7


def _qkv_kernel(x_ref, *refs):
    _qkv_body(x_ref[...], *refs)


def _qkv_in_specs(tm, layer):
    return [
        _layer_spec((1, D_MODEL), layer),
        _layer_spec((D_MODEL, QKV_WIDTH), layer),
        _layer_spec((1, LANES), layer),
        _layer_spec((1, LANES), layer),
        pl.BlockSpec((tm, LANES), lambda b, i: (i, 0)),
        pl.BlockSpec((tm, LANES), lambda b, i: (i, 0)),
        _const_spec((LANES, LANES)),
    ]


def _qkv_out_specs(tm):
    return [
        pl.BlockSpec((None, A_HEADS * LANES, tm), lambda b, i: (b, 0, i)),
        pl.BlockSpec((None, tm, A_KV_WIDTH), lambda b, i: (b, i, 0)),
        pl.BlockSpec((None, A_KV_WIDTH, tm), lambda b, i: (b, 0, i)),
    ]


def _qkv_out_shapes(bsz, seq):
    return [
        jax.ShapeDtypeStruct((bsz, A_HEADS * LANES, seq), BF16),
        jax.ShapeDtypeStruct((bsz, seq, A_KV_WIDTH), BF16),
        jax.ShapeDtypeStruct((bsz, A_KV_WIDTH, seq), BF16),
    ]


def _qkv(x, layer, qkv_params):
    bsz, seq, _ = x.shape
    tm = QKV_TILE
    return pl.pallas_call(
        _qkv_kernel,
        grid=(bsz, seq // tm),
        in_specs=[pl.BlockSpec((None, tm, D_MODEL), lambda b, i: (b, i, 0))]
        + _qkv_in_specs(tm, layer),
        out_specs=_qkv_out_specs(tm),
        out_shape=_qkv_out_shapes(bsz, seq),
        compiler_params=pltpu.CompilerParams(
            dimension_semantics=("arbitrary", "arbitrary"),
            vmem_limit_bytes=VMEM_LIMIT_BYTES),
        name="qkv",
    )(x, *qkv_params)


def _attn_kernel(q_ref, k_ref, vT_ref, o_ref, s_even, s_odd):
    seq, tq = s_even.shape
    n_chunks = seq // ATTN_KV_CHUNK
    grp = A_HEADS // A_KV_HEADS

    def rows8(a):
        return a.reshape(a.shape[0] // 8, 8, tq)

    def stage(t_qk, s_qk, t_sm, s_sm, m_sm):
        if t_qk is not None:
            q = q_ref[t_qk]
            m8 = None
        if t_sm is not None:
            v_rows = pl.ds(pl.multiple_of((t_sm // grp) * A_HEAD_DIM, A_HEAD_DIM), A_HEAD_DIM)
            acc = jnp.zeros((A_HEAD_DIM, tq), F32)
            l8 = jnp.zeros((8, tq), F32)
        for c in range(n_chunks):
            kv = slice(c * ATTN_KV_CHUNK, (c + 1) * ATTN_KV_CHUNK)
            if t_qk is not None:
                sc = _dot(k_ref[kv, :], q)
                s_qk[kv, :] = sc
                cm = jnp.max(rows8(sc), axis=0)
                m8 = cm if m8 is None else jnp.maximum(m8, cm)
            if t_sm is not None:
                p = jnp.exp2(s_sm[kv, :] - m_sm)
                l8 = l8 + jnp.sum(rows8(p), axis=0)
                acc = acc + _dot(vT_ref[v_rows, kv], p.astype(BF16))
        if t_sm is not None:
            l = jnp.sum(l8, axis=0, keepdims=True)
            o_rows = pl.ds(pl.multiple_of(t_sm * A_HEAD_DIM, A_HEAD_DIM), A_HEAD_DIM)
            o_ref[o_rows, :] = acc * (1.0 / l)
        if t_qk is not None:
            return jnp.max(m8, axis=0, keepdims=True)
        return None

    m_first = stage(0, s_even, None, None, None)

    def pair(u, m_even):
        m_odd = stage(2 * u + 1, s_odd, 2 * u, s_even, m_even)
        return stage(2 * u + 2, s_even, 2 * u + 1, s_odd, m_odd)

    m_even = lax.fori_loop(0, A_HEADS // 2 - 1, pair, m_first)
    m_odd = stage(A_HEADS - 1, s_odd, A_HEADS - 2, s_even, m_even)
    stage(None, None, A_HEADS - 1, s_odd, m_odd)


def _attn(qT, k, vT):
    bsz, seq, _ = k.shape
    tq = ATTN_TILE
    return pl.pallas_call(
        _attn_kernel,
        grid=(bsz, seq // tq),
        in_specs=[
            pl.BlockSpec((None, A_HEADS, LANES, tq), lambda b, i: (b, 0, 0, i)),
            pl.BlockSpec((None, seq, A_KV_WIDTH), lambda b, i: (b, 0, 0)),
            pl.BlockSpec((None, A_KV_WIDTH, seq), lambda b, i: (b, 0, 0)),
        ],
        out_specs=pl.BlockSpec((None, A_WIDTH, tq), lambda b, i: (b, 0, i)),
        out_shape=jax.ShapeDtypeStruct((bsz, A_WIDTH, seq), F32),
        scratch_shapes=[pltpu.VMEM((seq, tq), F32), pltpu.VMEM((seq, tq), F32)],
        compiler_params=pltpu.CompilerParams(
            dimension_semantics=("arbitrary", "arbitrary"),
            vmem_limit_bytes=VMEM_LIMIT_BYTES),
        name="attn",
    )(qT.reshape(bsz, A_HEADS, LANES, seq), k, vT)


def _rest_kernel(x_ref, yAT_ref, kmem_ref, vmem_ref, g_ref, w_ref, lng_ref, lnb_ref,
                 ws_ref, bs_ref, wbr_ref, wout_ref, *refs, final):
    if final:
        fg_ref, o_ref = refs
    else:
        qkv_in, o_ref, qkv_out = refs[:N_QKV_PARAMS], refs[N_QKV_PARAMS], refs[N_QKV_PARAMS + 1:]
    tq = x_ref.shape[0]
    x = x_ref[...]
    h = _rms(x, g_ref[...]).astype(BF16)
    proj = _dot(h, w_ref[...])
    zA = proj[:, 0:512]
    uB = proj[:, 512:1024]
    vB = proj[:, 1024:1536]
    zB = proj[:, 1536:2048]
    qM = proj[:, 2048:2560]
    zM = proj[:, 2560:3072]

    yA = yAT_ref[...].T * jax.nn.silu(zA)

    mu = jnp.mean(vB, axis=-1, keepdims=True)
    vc = vB - mu
    vn = vc * lax.rsqrt(jnp.mean(vc * vc, axis=-1, keepdims=True) + EPS)
    vn = (vn * lng_ref[...] + lnb_ref[...]).astype(BF16)
    rows = []
    for c in range(tq // CHUNK):
        cols = []
        for g in range(B_GROUPS):
            blk = vn[c * CHUNK:(c + 1) * CHUNK, g * B_GROUP_DIM:(g + 1) * B_GROUP_DIM]
            cols.append(_dot(ws_ref[g], blk) + bs_ref[g])
        rows.append(jnp.concatenate(cols, axis=1))
    mixed = jnp.concatenate(rows, axis=0)
    yB = uB * mixed * jax.nn.silu(zB)

    heads = []
    for hh in range(M_HEADS):
        sl = slice(hh * M_HEAD_DIM, (hh + 1) * M_HEAD_DIM)
        s = lax.dot_general(qM[:, sl].astype(BF16), kmem_ref[:, sl],
                            (((1,), (1,)), ((), ())), preferred_element_type=F32)
        p = jnp.exp2(s - jnp.max(s, axis=-1, keepdims=True))
        l = jnp.sum(p, axis=-1, keepdims=True)
        heads.append(_dot(p.astype(BF16), vmem_ref[:, sl]) * (1.0 / l))
    yM = jnp.concatenate(heads, axis=1) * jax.nn.silu(zM)

    merged = None
    for n, y in enumerate((yA, yB, yM)):
        up = _dot(y.astype(BF16), wbr_ref[n])
        gate = jax.nn.sigmoid(proj[:, 3072 + n * D_MODEL:3072 + (n + 1) * D_MODEL])
        merged = gate * up if merged is None else merged + gate * up
    out = x + _dot(merged.astype(BF16), wout_ref[...])
    if final:
        o_ref[...] = _rms(out, fg_ref[...])
    else:
        o_ref[...] = out
        _qkv_body(out, *qkv_in, *qkv_out)


def _rest(x, yAT, kmem, vmem, layer, rest_params, tail_params, final):
    bsz, seq, _ = x.shape
    tq = REST_TILE
    x_spec = pl.BlockSpec((None, tq, D_MODEL), lambda b, i: (b, i, 0))
    x_shape = jax.ShapeDtypeStruct((bsz, seq, D_MODEL), F32)
    if final:
        tail_specs = [_const_spec((1, D_MODEL))]
        out_specs, out_shape = x_spec, x_shape
    else:
        tail_specs = _qkv_in_specs(tq, layer + 1)
        out_specs = [x_spec] + _qkv_out_specs(tq)
        out_shape = [x_shape] + _qkv_out_shapes(bsz, seq)
    return pl.pallas_call(
        functools.partial(_rest_kernel, final=final),
        grid=(bsz, seq // tq),
        in_specs=[
            pl.BlockSpec((None, tq, D_MODEL), lambda b, i: (b, i, 0)),
            pl.BlockSpec((None, A_WIDTH, tq), lambda b, i: (b, 0, i)),
            pl.BlockSpec((None, None, MEM_LEN, M_WIDTH), lambda b, i: (layer, b, 0, 0)),
            pl.BlockSpec((None, None, MEM_LEN, M_WIDTH), lambda b, i: (layer, b, 0, 0)),
            _layer_spec((1, D_MODEL), layer),
            _layer_spec((D_MODEL, REST_WIDTH), layer),
            _layer_spec((1, B_WIDTH), layer),
            _layer_spec((1, B_WIDTH), layer),
            _layer_spec((B_GROUPS, CHUNK, CHUNK), layer),
            _layer_spec((B_GROUPS, CHUNK, B_GROUP_DIM), layer),
            _layer_spec((N_BRANCH, 512, D_MODEL), layer),
            _layer_spec((D_MODEL, D_MODEL), layer),
        ] + tail_specs,
        out_specs=out_specs,
        out_shape=out_shape,
        compiler_params=pltpu.CompilerParams(
            dimension_semantics=("arbitrary", "arbitrary"),
            vmem_limit_bytes=VMEM_LIMIT_BYTES),
        name="rest",
    )(x, yAT, kmem, vmem, *rest_params, *tail_params)


def _rope_tables(seq):
    rows = seq // GRID_W
    row = jnp.repeat(jnp.arange(rows, dtype=F32), GRID_W)
    col = jnp.tile(jnp.arange(GRID_W, dtype=F32), rows)
    n_freq = A_HEAD_DIM // 4
    inv = ROPE_THETA ** (-jnp.arange(n_freq, dtype=F32) / n_freq)
    ang = jnp.stack([row[:, None] * inv, col[:, None] * inv], axis=1)
    cos, sin = jnp.cos(ang), jnp.sin(ang)
    cos_hd = jnp.broadcast_to(cos[:, :, None, :], (seq, 2, 2, n_freq)).reshape(seq, A_HEAD_DIM)
    sin_hd = jnp.stack([-sin, sin], axis=2).reshape(seq, A_HEAD_DIM)
    reps = LANES // A_HEAD_DIM
    return jnp.tile(cos_hd, (1, reps)), jnp.tile(sin_hd, (1, reps))


def kernel(x, mem, norm_g, w_in, q_norm_g, k_norm_g, sg_ln_g, sg_ln_b, w_s, b_s,
           mem_norm_g, w_mem_kv, w_br, w_out, final_g):
    bsz, seq, _ = x.shape
    cos, sin = _rope_tables(seq)
    lane_head = jnp.arange(LANES) // A_HEAD_DIM
    ones = (lane_head[:, None] == lane_head[None, :]).astype(BF16)
    reps = LANES // A_HEAD_DIM
    q_scale = A_HEAD_DIM ** -0.5 * LOG2E

    w_qkv = w_in[:, :, :QKV_WIDTH].astype(BF16)
    w_rest = w_in[:, :, QKV_WIDTH:].astype(BF16)
    g = norm_g.reshape(DEPTH, 1, D_MODEL)
    gq = (jnp.tile(q_norm_g, (1, reps)) * q_scale).reshape(DEPTH, 1, LANES)
    gk = jnp.tile(k_norm_g, (1, reps)).reshape(DEPTH, 1, LANES)
    lng = sg_ln_g.reshape(DEPTH, 1, B_WIDTH)
    lnb = sg_ln_b.reshape(DEPTH, 1, B_WIDTH)
    ws = w_s.astype(BF16)
    bs = jnp.broadcast_to(b_s[:, :, :, None], (DEPTH, B_GROUPS, CHUNK, B_GROUP_DIM))
    wbr = w_br.astype(BF16)
    wout = w_out.astype(BF16)
    fg = final_g.reshape(1, D_MODEL)
    kmem, vmem = _mem_kv(mem, mem_norm_g, w_mem_kv.astype(BF16))
    qkv_params = (g, w_qkv, gq, gk, cos, sin, ones)
    assert len(qkv_params) == N_QKV_PARAMS
    rest_params = (g, w_rest, lng, lnb, ws, bs, wbr, wout)
    qT, k, vT = _qkv(x, 0, qkv_params)
    for l in range(DEPTH - 1):
        yAT = _attn(qT, k, vT)
        x, qT, k, vT = _rest(x, yAT, kmem, vmem, l, rest_params, qkv_params, final=False)
    yAT = _attn(qT, k, vT)
    return _rest(x, yAT, kmem, vmem, DEPTH - 1, rest_params, (fg,), final=True)
```

```python
import functools
import math

import jax
import jax.numpy as jnp
from jax import lax
from jax.experimental import pallas as pl
from jax.experimental.pallas import tpu as pltpu

D_MODEL = 1024
DEPTH = 4
GRID_W = 64
CHUNK = 128
ROPE_THETA = 10000.0
EPS = 1e-6
A_HEADS = 8
A_KV_HEADS = 2
A_HEAD_DIM = 64
A_WIDTH = A_HEADS * A_HEAD_DIM
A_KV_WIDTH = A_KV_HEADS * A_HEAD_DIM
B_GROUPS = 4
B_GROUP_DIM = 128
B_WIDTH = B_GROUPS * B_GROUP_DIM
M_HEADS = 4
M_HEAD_DIM = 128
M_WIDTH = M_HEADS * M_HEAD_DIM
MEM_LEN = 256
N_BRANCH = 3
QKV_WIDTH = A_WIDTH + 2 * A_KV_WIDTH
REST_WIDTH = 512 + 3 * 512 + 2 * 512 + N_BRANCH * D_MODEL

LANES = 128
LOG2E = math.log2(math.e)
VMEM_LIMIT_BYTES = 56 * 1024 * 1024

QKV_TILE = 512
ATTN_TILE = 512
ATTN_KV_CHUNK = 256
REST_TILE = 512

F32 = jnp.float32
BF16 = jnp.bfloat16


def _rms(x, g):
    return x * lax.rsqrt(jnp.mean(x * x, axis=-1, keepdims=True) + EPS) * g


def _dot(a, b):
    return jnp.dot(a, b, preferred_element_type=F32)


def _const_spec(shape):
    nd = len(shape)
    return pl.BlockSpec(shape, lambda *_: (0,) * nd, pipeline_mode=pl.Buffered(1))


def _layer_spec(shape, layer):
    nd = len(shape)
    return pl.BlockSpec((None,) + shape, lambda *_: (layer,) + (0,) * nd,
                        pipeline_mode=pl.Buffered(1))


def _mem_kernel(mem_ref, g_ref, w_ref, k_ref, v_ref):
    n = _rms(mem_ref[...], g_ref[...]).astype(BF16)
    kv = _dot(n, w_ref[...])
    k_ref[...] = (kv[:, :M_WIDTH] * (M_HEAD_DIM ** -0.5 * LOG2E)).astype(BF16)
    v_ref[...] = kv[:, M_WIDTH:].astype(BF16)


def _mem_kv(mem, mem_norm_g, w_mem_kv):
    bsz = mem.shape[0]
    out = jax.ShapeDtypeStruct((DEPTH, bsz, MEM_LEN, M_WIDTH), BF16)
    return pl.pallas_call(
        _mem_kernel,
        grid=(DEPTH, bsz),
        in_specs=[
            pl.BlockSpec((None, MEM_LEN, D_MODEL), lambda l, b: (b, 0, 0)),
            pl.BlockSpec((None, 1, D_MODEL), lambda l, b: (l, 0, 0)),
            pl.BlockSpec((None, D_MODEL, 2 * M_WIDTH), lambda l, b: (l, 0, 0)),
        ],
        out_specs=[
            pl.BlockSpec((None, None, MEM_LEN, M_WIDTH), lambda l, b: (l, b, 0, 0)),
            pl.BlockSpec((None, None, MEM_LEN, M_WIDTH), lambda l, b: (l, b, 0, 0)),
        ],
        out_shape=[out, out],
        compiler_params=pltpu.CompilerParams(
            dimension_semantics=("arbitrary", "arbitrary"),
            vmem_limit_bytes=VMEM_LIMIT_BYTES),
        name="mem_kv",
    )(mem, mem_norm_g.reshape(DEPTH, 1, D_MODEL), w_mem_kv)


def _qkv_body(x, g_ref, w_ref, gq_ref, gk_ref, cos_ref, sin_ref, ones_ref,
              qT_ref, k_ref, vT_ref):
    tm = x.shape[0]
    h = _rms(x, g_ref[...]).astype(BF16)
    qkv = _dot(h, w_ref[...])
    cos = cos_ref[...]
    sin = sin_ref[...]
    ones = ones_ref[...]
    lane = lax.broadcasted_iota(jnp.int32, (tm, LANES), 1)
    first_half = (lane & 16) == 0

    def norm_rope(t, gvec):
        sq = t * t
        hi = sq.astype(BF16)
        lo = (sq - hi.astype(F32)).astype(BF16)
        ms = (_dot(hi, ones) + _dot(lo, ones)) * (1.0 / A_HEAD_DIM)
        tn = t * lax.rsqrt(ms + EPS) * gvec
        partner = jnp.where(first_half,
                            pltpu.roll(tn, LANES - 16, 1),
                            pltpu.roll(tn, 16, 1))
        return tn * cos + partner * sin

    zeros = jnp.zeros((A_HEAD_DIM, tm), BF16)
    gq = gq_ref[...]
    for i in range(A_WIDTH // LANES):
        qt = norm_rope(qkv[:, i * LANES:(i + 1) * LANES], gq).T.astype(BF16)
        kv_head = (2 * i) // (A_HEADS // A_KV_HEADS)
        for half in range(2):
            base = (2 * i + half) * LANES
            piece = qt[half * A_HEAD_DIM:(half + 1) * A_HEAD_DIM, :]
            lo_rows = pl.ds(base, A_HEAD_DIM)
            hi_rows = pl.ds(base + A_HEAD_DIM, A_HEAD_DIM)
            if kv_head == 0:
                qT_ref[lo_rows, :] = piece
                qT_ref[hi_rows, :] = zeros
            else:
                qT_ref[lo_rows, :] = zeros
                qT_ref[hi_rows, :] = piece
    k_ref[...] = norm_rope(qkv[:, A_WIDTH:A_WIDTH + A_KV_WIDTH], gk_ref[...]).astype(BF16)
    vT_ref[...] = qkv[:, A_WIDTH + A_KV_WIDTH:].T.astype(BF16)


N_QKV_PARAMS = 7


def _qkv_kernel(x_ref, *refs):
    _qkv_body(x_ref[...], *refs)


def _qkv_in_specs(tm, layer):
    return [
        _layer_spec((1, D_MODEL), layer),
        _layer_spec((D_MODEL, QKV_WIDTH), layer),
        _layer_spec((1, LANES), layer),
        _layer_spec((1, LANES), layer),
        pl.BlockSpec((tm, LANES), lambda b, i: (i, 0)),
        pl.BlockSpec((tm, LANES), lambda b, i: (i, 0)),
        _const_spec((LANES, LANES)),
    ]


def _qkv_out_specs(tm):
    return [
        pl.BlockSpec((None, A_HEADS * LANES, tm), lambda b, i: (b, 0, i)),
        pl.BlockSpec((None, tm, A_KV_WIDTH), lambda b, i: (b, i, 0)),
        pl.BlockSpec((None, A_KV_WIDTH, tm), lambda b, i: (b, 0, i)),
    ]


def _qkv_out_shapes(bsz, seq):
    return [
        jax.ShapeDtypeStruct((bsz, A_HEADS * LANES, seq), BF16),
        jax.ShapeDtypeStruct((bsz, seq, A_KV_WIDTH), BF16),
        jax.ShapeDtypeStruct((bsz, A_KV_WIDTH, seq), BF16),
    ]


def _qkv(x, layer, qkv_params):
    bsz, seq, _ = x.shape
    tm = QKV_TILE
    return pl.pallas_call(
        _qkv_kernel,
        grid=(bsz, seq // tm),
        in_specs=[pl.BlockSpec((None, tm, D_MODEL), lambda b, i: (b, i, 0))]
        + _qkv_in_specs(tm, layer),
        out_specs=_qkv_out_specs(tm),
        out_shape=_qkv_out_shapes(bsz, seq),
        compiler_params=pltpu.CompilerParams(
            dimension_semantics=("arbitrary", "arbitrary"),
            vmem_limit_bytes=VMEM_LIMIT_BYTES),
        name="qkv",
    )(x, *qkv_params)


def _attn_kernel(q_ref, k_ref, vT_ref, qn_ref, kn_ref, o_ref, s_even, s_odd, m_ref):
    seq, tq = s_even.shape
    n_chunks = seq // ATTN_KV_CHUNK
    grp = A_HEADS // A_KV_HEADS

    def rows8(a):
        return a.reshape(a.shape[0] // 8, 8, tq)

    def stage(q, k_src, s_qk, t_sm, s_sm, m_sm):
        m8 = None
        if t_sm is not None:
            v_rows = pl.ds(pl.multiple_of((t_sm // grp) * A_HEAD_DIM, A_HEAD_DIM), A_HEAD_DIM)
            acc = jnp.zeros((A_HEAD_DIM, tq), F32)
            l8 = jnp.zeros((8, tq), F32)
        for c in range(n_chunks):
            kv = slice(c * ATTN_KV_CHUNK, (c + 1) * ATTN_KV_CHUNK)
            sc = _dot(k_src[kv, :], q)
            s_qk[kv, :] = sc
            cm = jnp.max(rows8(sc), axis=0)
            m8 = cm if m8 is None else jnp.maximum(m8, cm)
            if t_sm is not None:
                p = jnp.exp2(s_sm[kv, :] - m_sm)
                l8 = l8 + jnp.sum(rows8(p), axis=0)
                acc = acc + _dot(vT_ref[v_rows, kv], p.astype(BF16))
        if t_sm is not None:
            l = jnp.sum(l8, axis=0, keepdims=True)
            o_rows = pl.ds(pl.multiple_of(t_sm * A_HEAD_DIM, A_HEAD_DIM), A_HEAD_DIM)
            o_ref[o_rows, :] = acc * (1.0 / l)
        return jnp.max(m8, axis=0, keepdims=True)

    @pl.when((pl.program_id(0) == 0) & (pl.program_id(1) == 0))
    def _():
        m_ref[...] = stage(q_ref[0], k_ref, s_even, None, None, None)

    def pair(u, m_even):
        m_odd = stage(q_ref[2 * u + 1], k_ref, s_odd, 2 * u, s_even, m_even)
        return stage(q_ref[2 * u + 2], k_ref, s_even, 2 * u + 1, s_odd, m_odd)

    m_even = lax.fori_loop(0, A_HEADS // 2 - 1, pair, m_ref[...])
    m_odd = stage(q_ref[A_HEADS - 1], k_ref, s_odd, A_HEADS - 2, s_even, m_even)
    m_ref[...] = stage(qn_ref[...], kn_ref, s_even, A_HEADS - 1, s_odd, m_odd)


def _attn(qT, k, vT):
    bsz, seq, _ = k.shape
    tq = ATTN_TILE
    n_i = seq // tq

    def next_step(b, i):
        nxt = jnp.minimum(b * n_i + i + 1, bsz * n_i - 1)
        return nxt // n_i, nxt % n_i

    def qn_map(b, i):
        nb, ni = next_step(b, i)
        return nb, 0, 0, ni

    def kn_map(b, i):
        return next_step(b, i)[0], 0, 0

    q4 = qT.reshape(bsz, A_HEADS, LANES, seq)
    return pl.pallas_call(
        _attn_kernel,
        grid=(bsz, n_i),
        in_specs=[
            pl.BlockSpec((None, A_HEADS, LANES, tq), lambda b, i: (b, 0, 0, i)),
            pl.BlockSpec((None, seq, A_KV_WIDTH), lambda b, i: (b, 0, 0)),
            pl.BlockSpec((None, A_KV_WIDTH, seq), lambda b, i: (b, 0, 0)),
            pl.BlockSpec((None, None, LANES, tq), qn_map),
            pl.BlockSpec((None, seq, A_KV_WIDTH), kn_map),
        ],
        out_specs=pl.BlockSpec((None, A_WIDTH, tq), lambda b, i: (b, 0, i)),
        out_shape=jax.ShapeDtypeStruct((bsz, A_WIDTH, seq), F32),
        scratch_shapes=[pltpu.VMEM((seq, tq), F32), pltpu.VMEM((seq, tq), F32),
                        pltpu.VMEM((1, tq), F32)],
        compiler_params=pltpu.CompilerParams(
            dimension_semantics=("arbitrary", "arbitrary"),
            vmem_limit_bytes=VMEM_LIMIT_BYTES),
        name="attn",
    )(q4, k, vT, q4, k)


def _rest_kernel(x_ref, yAT_ref, kmem_ref, vmem_ref, g_ref, w_ref, lng_ref, lnb_ref,
                 ws_ref, bs_ref, wbr_ref, wout_ref, *refs, final):
    if final:
        fg_ref, o_ref = refs
    else:
        qkv_in, o_ref, qkv_out = refs[:N_QKV_PARAMS], refs[N_QKV_PARAMS], refs[N_QKV_PARAMS + 1:]
    tq = x_ref.shape[0]
    x = x_ref[...]
    h = _rms(x, g_ref[...]).astype(BF16)
    proj = _dot(h, w_ref[...])
    zA = proj[:, 0:512]
    uB = proj[:, 512:1024]
    vB = proj[:, 1024:1536]
    zB = proj[:, 1536:2048]
    qM = proj[:, 2048:2560]
    zM = proj[:, 2560:3072]

    yA = yAT_ref[...].T * jax.nn.silu(zA)

    mu = jnp.mean(vB, axis=-1, keepdims=True)
    vc = vB - mu
    vn = vc * lax.rsqrt(jnp.mean(vc * vc, axis=-1, keepdims=True) + EPS)
    vn = (vn * lng_ref[...] + lnb_ref[...]).astype(BF16)
    rows = []
    for c in range(tq // CHUNK):
        cols = []
        for g in range(B_GROUPS):
            blk = vn[c * CHUNK:(c + 1) * CHUNK, g * B_GROUP_DIM:(g + 1) * B_GROUP_DIM]
            cols.append(_dot(ws_ref[g], blk) + bs_ref[g])
        rows.append(jnp.concatenate(cols, axis=1))
    mixed = jnp.concatenate(rows, axis=0)
    yB = uB * mixed * jax.nn.silu(zB)

    heads = []
    for hh in range(M_HEADS):
        sl = slice(hh * M_HEAD_DIM, (hh + 1) * M_HEAD_DIM)
        s = lax.dot_general(qM[:, sl].astype(BF16), kmem_ref[:, sl],
                            (((1,), (1,)), ((), ())), preferred_element_type=F32)
        p = jnp.exp2(s - jnp.max(s, axis=-1, keepdims=True))
        l = jnp.sum(p, axis=-1, keepdims=True)
        heads.append(_dot(p.astype(BF16), vmem_ref[:, sl]) * (1.0 / l))
    yM = jnp.concatenate(heads, axis=1) * jax.nn.silu(zM)

    merged = None
    for n, y in enumerate((yA, yB, yM)):
        up = _dot(y.astype(BF16), wbr_ref[n])
        gate = jax.nn.sigmoid(proj[:, 3072 + n * D_MODEL:3072 + (n + 1) * D_MODEL])
        merged = gate * up if merged is None else merged + gate * up
    out = x + _dot(merged.astype(BF16), wout_ref[...])
    if final:
        o_ref[...] = _rms(out, fg_ref[...])
    else:
        o_ref[...] = out
        _qkv_body(out, *qkv_in, *qkv_out)


def _rest(x, yAT, kmem, vmem, layer, rest_params, tail_params, final):
    bsz, seq, _ = x.shape
    tq = REST_TILE
    x_spec = pl.BlockSpec((None, tq, D_MODEL), lambda b, i: (b, i, 0))
    x_shape = jax.ShapeDtypeStruct((bsz, seq, D_MODEL), F32)
    if final:
        tail_specs = [_const_spec((1, D_MODEL))]
        out_specs, out_shape = x_spec, x_shape
    else:
        tail_specs = _qkv_in_specs(tq, layer + 1)
        out_specs = [x_spec] + _qkv_out_specs(tq)
        out_shape = [x_shape] + _qkv_out_shapes(bsz, seq)
    return pl.pallas_call(
        functools.partial(_rest_kernel, final=final),
        grid=(bsz, seq // tq),
        in_specs=[
            pl.BlockSpec((None, tq, D_MODEL), lambda b, i: (b, i, 0)),
            pl.BlockSpec((None, A_WIDTH, tq), lambda b, i: (b, 0, i)),
            pl.BlockSpec((None, None, MEM_LEN, M_WIDTH), lambda b, i: (layer, b, 0, 0)),
            pl.BlockSpec((None, None, MEM_LEN, M_WIDTH), lambda b, i: (layer, b, 0, 0)),
            _layer_spec((1, D_MODEL), layer),
            _layer_spec((D_MODEL, REST_WIDTH), layer),
            _layer_spec((1, B_WIDTH), layer),
            _layer_spec((1, B_WIDTH), layer),
            _layer_spec((B_GROUPS, CHUNK, CHUNK), layer),
            _layer_spec((B_GROUPS, CHUNK, B_GROUP_DIM), layer),
            _layer_spec((N_BRANCH, 512, D_MODEL), layer),
            _layer_spec((D_MODEL, D_MODEL), layer),
        ] + tail_specs,
        out_specs=out_specs,
        out_shape=out_shape,
        compiler_params=pltpu.CompilerParams(
            dimension_semantics=("arbitrary", "arbitrary"),
            vmem_limit_bytes=VMEM_LIMIT_BYTES),
        name="rest",
    )(x, yAT, kmem, vmem, *rest_params, *tail_params)


def _rope_tables(seq):
    rows = seq // GRID_W
    row = jnp.repeat(jnp.arange(rows, dtype=F32), GRID_W)
    col = jnp.tile(jnp.arange(GRID_W, dtype=F32), rows)
    n_freq = A_HEAD_DIM // 4
    inv = ROPE_THETA ** (-jnp.arange(n_freq, dtype=F32) / n_freq)
    ang = jnp.stack([row[:, None] * inv, col[:, None] * inv], axis=1)
    cos, sin = jnp.cos(ang), jnp.sin(ang)
    cos_hd = jnp.broadcast_to(cos[:, :, None, :], (seq, 2, 2, n_freq)).reshape(seq, A_HEAD_DIM)
    sin_hd = jnp.stack([-sin, sin], axis=2).reshape(seq, A_HEAD_DIM)
    reps = LANES // A_HEAD_DIM
    return jnp.tile(cos_hd, (1, reps)), jnp.tile(sin_hd, (1, reps))


def kernel(x, mem, norm_g, w_in, q_norm_g, k_norm_g, sg_ln_g, sg_ln_b, w_s, b_s,
           mem_norm_g, w_mem_kv, w_br, w_out, final_g):
    bsz, seq, _ = x.shape
    cos, sin = _rope_tables(seq)
    lane_head = jnp.arange(LANES) // A_HEAD_DIM
    ones = (lane_head[:, None] == lane_head[None, :]).astype(BF16)
    reps = LANES // A_HEAD_DIM
    q_scale = A_HEAD_DIM ** -0.5 * LOG2E

    w_qkv = w_in[:, :, :QKV_WIDTH].astype(BF16)
    w_rest = w_in[:, :, QKV_WIDTH:].astype(BF16)
    g = norm_g.reshape(DEPTH, 1, D_MODEL)
    gq = (jnp.tile(q_norm_g, (1, reps)) * q_scale).reshape(DEPTH, 1, LANES)
    gk = jnp.tile(k_norm_g, (1, reps)).reshape(DEPTH, 1, LANES)
    lng = sg_ln_g.reshape(DEPTH, 1, B_WIDTH)
    lnb = sg_ln_b.reshape(DEPTH, 1, B_WIDTH)
    ws = w_s.astype(BF16)
    bs = jnp.broadcast_to(b_s[:, :, :, None], (DEPTH, B_GROUPS, CHUNK, B_GROUP_DIM))
    wbr = w_br.astype(BF16)
    wout = w_out.astype(BF16)
    fg = final_g.reshape(1, D_MODEL)
    kmem, vmem = _mem_kv(mem, mem_norm_g, w_mem_kv.astype(BF16))
    qkv_params = (g, w_qkv, gq, gk, cos, sin, ones)
    assert len(qkv_params) == N_QKV_PARAMS
    rest_params = (g, w_rest, lng, lnb, ws, bs, wbr, wout)
    qT, k, vT = _qkv(x, 0, qkv_params)
    for l in range(DEPTH - 1):
        yAT = _attn(qT, k, vT)
        x, qT, k, vT = _rest(x, yAT, kmem, vmem, l, rest_params, qkv_params, final=False)
    yAT = _attn(qT, k, vT)
    return _rest(x, yAT, kmem, vmem, DEPTH - 1, rest_params, (fg,), final=True)
```

```python
import functools
import math

import jax
import jax.numpy as jnp
from jax import lax
from jax.experimental import pallas as pl
from jax.experimental.pallas import tpu as pltpu

D_MODEL = 1024
DEPTH = 4
GRID_W = 64
CHUNK = 128
ROPE_THETA = 10000.0
EPS = 1e-6
A_HEADS = 8
A_KV_HEADS = 2
A_HEAD_DIM = 64
A_WIDTH = A_HEADS * A_HEAD_DIM
A_KV_WIDTH = A_KV_HEADS * A_HEAD_DIM
B_GROUPS = 4
B_GROUP_DIM = 128
B_WIDTH = B_GROUPS * B_GROUP_DIM
M_HEADS = 4
M_HEAD_DIM = 128
M_WIDTH = M_HEADS * M_HEAD_DIM
MEM_LEN = 256
N_BRANCH = 3
QKV_WIDTH = A_WIDTH + 2 * A_KV_WIDTH
REST_WIDTH = 512 + 3 * 512 + 2 * 512 + N_BRANCH * D_MODEL
IN_WIDTH = QKV_WIDTH + REST_WIDTH

LANES = 128
ONES_WIDTH = 256
LOG2E = math.log2(math.e)
VMEM_LIMIT_BYTES = 56 * 1024 * 1024

QKV_TILE = 512
ATTN_TILE = 512
ATTN_KV_CHUNK = 256
REST_TILE = 512

F32 = jnp.float32
BF16 = jnp.bfloat16


def _rms(x, g):
    return x * lax.rsqrt(jnp.mean(x * x, axis=-1, keepdims=True) + EPS) * g


def _dot(a, b):
    return jnp.dot(a, b, preferred_element_type=F32)


def _half_silu(hz):
    return hz + hz * jnp.tanh(hz)


def _const_spec(shape):
    nd = len(shape)
    return pl.BlockSpec(shape, lambda *_: (0,) * nd, pipeline_mode=pl.Buffered(1))


def _layer_spec(shape, layer):
    nd = len(shape)
    return pl.BlockSpec((None,) + shape, lambda *_: (layer,) + (0,) * nd,
                        pipeline_mode=pl.Buffered(1))


def _mem_kernel(mem_ref, g_ref, w_ref, k_ref, v_ref):
    n = _rms(mem_ref[...], g_ref[...]).astype(BF16)
    kv = _dot(n, w_ref[...])
    k_ref[...] = (kv[:, :M_WIDTH] * (M_HEAD_DIM ** -0.5 * LOG2E)).astype(BF16)
    v_ref[...] = kv[:, M_WIDTH:].astype(BF16)


def _mem_kv(mem, mem_norm_g, w_mem_kv):
    bsz = mem.shape[0]
    out = jax.ShapeDtypeStruct((DEPTH, bsz, MEM_LEN, M_WIDTH), BF16)
    return pl.pallas_call(
        _mem_kernel,
        grid=(DEPTH, bsz),
        in_specs=[
            pl.BlockSpec((None, MEM_LEN, D_MODEL), lambda l, b: (b, 0, 0)),
            pl.BlockSpec((None, 1, D_MODEL), lambda l, b: (l, 0, 0)),
            pl.BlockSpec((None, D_MODEL, 2 * M_WIDTH), lambda l, b: (l, 0, 0)),
        ],
        out_specs=[
            pl.BlockSpec((None, None, MEM_LEN, M_WIDTH), lambda l, b: (l, b, 0, 0)),
            pl.BlockSpec((None, None, MEM_LEN, M_WIDTH), lambda l, b: (l, b, 0, 0)),
        ],
        out_shape=[out, out],
        compiler_params=pltpu.CompilerParams(
            dimension_semantics=("arbitrary", "arbitrary"),
            vmem_limit_bytes=VMEM_LIMIT_BYTES),
        name="mem_kv",
    )(mem, mem_norm_g.reshape(DEPTH, 1, D_MODEL), w_mem_kv)


def _qkv_body(x, g_ref, w_ref, gq_ref, gk_ref, cos_ref, sin_ref, ones_ref,
              qT_ref, k_ref, vT_ref):
    tm = x.shape[0]
    h = _rms(x, g_ref[...]).astype(BF16)
    qkv = _dot(h, w_ref[...])
    cos = cos_ref[...]
    sin = sin_ref[...]
    ones = ones_ref[...]
    lane = lax.broadcasted_iota(jnp.int32, (tm, LANES), 1)
    first_half = (lane & 16) == 0

    def head_ms(t):
        w = t.shape[1]
        return _dot((t * t).astype(BF16), ones[:w, :w]) * (1.0 / A_HEAD_DIM)

    def norm_rope(t, ms, gvec):
        tn = t * lax.rsqrt(ms + EPS) * gvec
        partner = jnp.where(first_half,
                            pltpu.roll(tn, LANES - 16, 1),
                            pltpu.roll(tn, 16, 1))
        return tn * cos + partner * sin

    zeros = jnp.zeros((A_HEAD_DIM, tm), BF16)
    gq = gq_ref[...]
    ones_w = ones.shape[0]
    q_ms = jnp.concatenate([head_ms(qkv[:, c:c + ones_w]) for c in range(0, A_WIDTH, ones_w)],
                           axis=1)
    for i in range(A_WIDTH // LANES):
        sl = slice(i * LANES, (i + 1) * LANES)
        qt = norm_rope(qkv[:, sl], q_ms[:, sl], gq).T.astype(BF16)
        kv_head = (2 * i) // (A_HEADS // A_KV_HEADS)
        for half in range(2):
            base = (2 * i + half) * LANES
            piece = qt[half * A_HEAD_DIM:(half + 1) * A_HEAD_DIM, :]
            lo_rows = pl.ds(base, A_HEAD_DIM)
            hi_rows = pl.ds(base + A_HEAD_DIM, A_HEAD_DIM)
            if kv_head == 0:
                qT_ref[lo_rows, :] = piece
                qT_ref[hi_rows, :] = zeros
            else:
                qT_ref[lo_rows, :] = zeros
                qT_ref[hi_rows, :] = piece
    kk = qkv[:, A_WIDTH:A_WIDTH + A_KV_WIDTH]
    k_ref[...] = norm_rope(kk, head_ms(kk), gk_ref[...]).astype(BF16)
    vT_ref[...] = qkv[:, A_WIDTH + A_KV_WIDTH:].T.astype(BF16)


N_QKV_PARAMS = 7


def _qkv_kernel(x_ref, *refs):
    _qkv_body(x_ref[...], *refs)


def _qkv_in_specs(tm, layer):
    return [
        _layer_spec((1, D_MODEL), layer),
        _layer_spec((D_MODEL, QKV_WIDTH), layer),
        _layer_spec((1, LANES), layer),
        _layer_spec((1, LANES), layer),
        pl.BlockSpec((tm, LANES), lambda b, i: (i, 0)),
        pl.BlockSpec((tm, LANES), lambda b, i: (i, 0)),
        _const_spec((ONES_WIDTH, ONES_WIDTH)),
    ]


def _qkv_out_specs(tm):
    return [
        pl.BlockSpec((None, A_HEADS * LANES, tm), lambda b, i: (b, 0, i)),
        pl.BlockSpec((None, tm, A_KV_WIDTH), lambda b, i: (b, i, 0)),
        pl.BlockSpec((None, A_KV_WIDTH, tm), lambda b, i: (b, 0, i)),
    ]


def _qkv_out_shapes(bsz, seq):
    return [
        jax.ShapeDtypeStruct((bsz, A_HEADS * LANES, seq), BF16),
        jax.ShapeDtypeStruct((bsz, seq, A_KV_WIDTH), BF16),
        jax.ShapeDtypeStruct((bsz, A_KV_WIDTH, seq), BF16),
    ]


def _qkv(x, layer, qkv_params):
    bsz, seq, _ = x.shape
    tm = QKV_TILE
    return pl.pallas_call(
        _qkv_kernel,
        grid=(bsz, seq // tm),
        in_specs=[pl.BlockSpec((None, tm, D_MODEL), lambda b, i: (b, i, 0))]
        + _qkv_in_specs(tm, layer),
        out_specs=_qkv_out_specs(tm),
        out_shape=_qkv_out_shapes(bsz, seq),
        compiler_params=pltpu.CompilerParams(
            dimension_semantics=("arbitrary", "arbitrary"),
            vmem_limit_bytes=VMEM_LIMIT_BYTES),
        name="qkv",
    )(x, *qkv_params)


def _attn_kernel(q_ref, k_ref, vT_ref, qn_ref, kn_ref, o_ref, s_even, s_odd, m_ref):
    seq, tq = s_even.shape
    n_chunks = seq // ATTN_KV_CHUNK
    grp = A_HEADS // A_KV_HEADS

    def rows8(a):
        return a.reshape(a.shape[0] // 8, 8, tq)

    def stage(q, k_src, s_qk, t_sm, s_sm, m_sm):
        m8 = None
        if t_sm is not None:
            v_rows = pl.ds(pl.multiple_of((t_sm // grp) * A_HEAD_DIM, A_HEAD_DIM), A_HEAD_DIM)
            acc = jnp.zeros((A_HEAD_DIM, tq), F32)
            l8 = jnp.zeros((8, tq), F32)
        for c in range(n_chunks):
            kv = slice(c * ATTN_KV_CHUNK, (c + 1) * ATTN_KV_CHUNK)
            sc = _dot(k_src[kv, :], q)
            s_qk[kv, :] = sc
            cm = jnp.max(rows8(sc), axis=0)
            m8 = cm if m8 is None else jnp.maximum(m8, cm)
            if t_sm is not None:
                p = jnp.exp2(s_sm[kv, :] - m_sm)
                l8 = l8 + jnp.sum(rows8(p), axis=0)
                acc = acc + _dot(vT_ref[v_rows, kv], p.astype(BF16))
        if t_sm is not None:
            l = jnp.sum(l8, axis=0, keepdims=True)
            o_rows = pl.ds(pl.multiple_of(t_sm * A_HEAD_DIM, A_HEAD_DIM), A_HEAD_DIM)
            o_ref[o_rows, :] = acc * (1.0 / l)
        return jnp.max(m8, axis=0, keepdims=True)

    @pl.when((pl.program_id(0) == 0) & (pl.program_id(1) == 0))
    def _():
        m_ref[...] = stage(q_ref[0], k_ref, s_even, None, None, None)

    def pair(u, m_even):
        m_odd = stage(q_ref[2 * u + 1], k_ref, s_odd, 2 * u, s_even, m_even)
        return stage(q_ref[2 * u + 2], k_ref, s_even, 2 * u + 1, s_odd, m_odd)

    m_even = lax.fori_loop(0, A_HEADS // 2 - 1, pair, m_ref[...])
    m_odd = stage(q_ref[A_HEADS - 1], k_ref, s_odd, A_HEADS - 2, s_even, m_even)
    m_ref[...] = stage(qn_ref[...], kn_ref, s_even, A_HEADS - 1, s_odd, m_odd)


def _attn(qT, k, vT):
    bsz, seq, _ = k.shape
    tq = ATTN_TILE
    n_i = seq // tq

    def next_step(b, i):
        nxt = jnp.minimum(b * n_i + i + 1, bsz * n_i - 1)
        return nxt // n_i, nxt % n_i

    def qn_map(b, i):
        nb, ni = next_step(b, i)
        return nb, 0, 0, ni

    def kn_map(b, i):
        return next_step(b, i)[0], 0, 0

    q4 = qT.reshape(bsz, A_HEADS, LANES, seq)
    return pl.pallas_call(
        _attn_kernel,
        grid=(bsz, n_i),
        in_specs=[
            pl.BlockSpec((None, A_HEADS, LANES, tq), lambda b, i: (b, 0, 0, i)),
            pl.BlockSpec((None, seq, A_KV_WIDTH), lambda b, i: (b, 0, 0)),
            pl.BlockSpec((None, A_KV_WIDTH, seq), lambda b, i: (b, 0, 0)),
            pl.BlockSpec((None, None, LANES, tq), qn_map),
            pl.BlockSpec((None, seq, A_KV_WIDTH), kn_map),
        ],
        out_specs=pl.BlockSpec((None, A_WIDTH, tq), lambda b, i: (b, 0, i)),
        out_shape=jax.ShapeDtypeStruct((bsz, A_WIDTH, seq), F32),
        scratch_shapes=[pltpu.VMEM((seq, tq), F32), pltpu.VMEM((seq, tq), F32),
                        pltpu.VMEM((1, tq), F32)],
        compiler_params=pltpu.CompilerParams(
            dimension_semantics=("arbitrary", "arbitrary"),
            vmem_limit_bytes=VMEM_LIMIT_BYTES),
        name="attn",
    )(q4, k, vT, q4, k)


def _rest_kernel(x_ref, yAT_ref, kmem_ref, vmem_ref, g_ref, w_ref, lng_ref, lnb_ref,
                 ws_ref, bs_ref, wbr_ref, wout_ref, *refs, final):
    if final:
        fg_ref, o_ref = refs
    else:
        qkv_in, o_ref, qkv_out = refs[:N_QKV_PARAMS], refs[N_QKV_PARAMS], refs[N_QKV_PARAMS + 1:]
    tq = x_ref.shape[0]
    x = x_ref[...]
    h = _rms(x, g_ref[...]).astype(BF16)
    proj = _dot(h, w_ref[:, QKV_WIDTH:])
    hzA = proj[:, 0:512]
    uB = proj[:, 512:1024]
    vB = proj[:, 1024:1536]
    hzB = proj[:, 1536:2048]
    qM = proj[:, 2048:2560]
    hzM = proj[:, 2560:3072]

    yA = yAT_ref[...].T * _half_silu(hzA)

    mu = jnp.mean(vB, axis=-1, keepdims=True)
    vc = vB - mu
    vn = vc * lax.rsqrt(jnp.mean(vc * vc, axis=-1, keepdims=True) + EPS)
    vn = (vn * lng_ref[...] + lnb_ref[...]).astype(BF16)
    n_chunk = tq // CHUNK
    per_group = []
    for g in range(B_GROUPS):
        gcols = slice(g * B_GROUP_DIM, (g + 1) * B_GROUP_DIM)
        wide = jnp.concatenate([vn[c * CHUNK:(c + 1) * CHUNK, gcols] for c in range(n_chunk)],
                               axis=1)
        per_group.append(_dot(ws_ref[g], wide))
    bias = jnp.concatenate([bs_ref[g] for g in range(B_GROUPS)], axis=1)
    mixed = jnp.concatenate(
        [jnp.concatenate([per_group[g][:, c * B_GROUP_DIM:(c + 1) * B_GROUP_DIM]
                          for g in range(B_GROUPS)], axis=1) + bias
         for c in range(n_chunk)], axis=0)
    yB = uB * mixed * _half_silu(hzB)

    heads = []
    for hh in range(M_HEADS):
        sl = slice(hh * M_HEAD_DIM, (hh + 1) * M_HEAD_DIM)
        s = lax.dot_general(qM[:, sl].astype(BF16), kmem_ref[:, sl],
                            (((1,), (1,)), ((), ())), preferred_element_type=F32)
        p = jnp.exp2(s - jnp.max(s, axis=-1, keepdims=True))
        l = jnp.sum(p, axis=-1, keepdims=True)
        heads.append(_dot(p.astype(BF16), vmem_ref[:, sl]) * (1.0 / l))
    yM = jnp.concatenate(heads, axis=1) * _half_silu(hzM)

    merged2 = None
    for n, y in enumerate((yA, yB, yM)):
        up = _dot(y.astype(BF16), wbr_ref[n])
        gate2 = 1.0 + jnp.tanh(proj[:, 3072 + n * D_MODEL:3072 + (n + 1) * D_MODEL])
        merged2 = gate2 * up if merged2 is None else merged2 + gate2 * up
    out = x + _dot(merged2.astype(BF16), wout_ref[...])
    if final:
        o_ref[...] = _rms(out, fg_ref[...])
    else:
        o_ref[...] = out
        _qkv_body(out, *qkv_in, *qkv_out)


def _rest(x, yAT, kmem, vmem, layer, rest_params, tail_params, final):
    bsz, seq, _ = x.shape
    tq = REST_TILE
    x_spec = pl.BlockSpec((None, tq, D_MODEL), lambda b, i: (b, i, 0))
    x_shape = jax.ShapeDtypeStruct((bsz, seq, D_MODEL), F32)
    if final:
        tail_specs = [_const_spec((1, D_MODEL))]
        out_specs, out_shape = x_spec, x_shape
    else:
        tail_specs = _qkv_in_specs(tq, layer + 1)
        out_specs = [x_spec] + _qkv_out_specs(tq)
        out_shape = [x_shape] + _qkv_out_shapes(bsz, seq)
    return pl.pallas_call(
        functools.partial(_rest_kernel, final=final),
        grid=(bsz, seq // tq),
        in_specs=[
            pl.BlockSpec((None, tq, D_MODEL), lambda b, i: (b, i, 0)),
            pl.BlockSpec((None, A_WIDTH, tq), lambda b, i: (b, 0, i)),
            pl.BlockSpec((None, None, MEM_LEN, M_WIDTH), lambda b, i: (layer, b, 0, 0)),
            pl.BlockSpec((None, None, MEM_LEN, M_WIDTH), lambda b, i: (layer, b, 0, 0)),
            _layer_spec((1, D_MODEL), layer),
            _layer_spec((D_MODEL, IN_WIDTH), layer),
            _layer_spec((1, B_WIDTH), layer),
            _layer_spec((1, B_WIDTH), layer),
            _layer_spec((B_GROUPS, CHUNK, CHUNK), layer),
            _layer_spec((B_GROUPS, CHUNK, B_GROUP_DIM), layer),
            _layer_spec((N_BRANCH, 512, D_MODEL), layer),
            _layer_spec((D_MODEL, D_MODEL), layer),
        ] + tail_specs,
        out_specs=out_specs,
        out_shape=out_shape,
        compiler_params=pltpu.CompilerParams(
            dimension_semantics=("arbitrary", "arbitrary"),
            vmem_limit_bytes=VMEM_LIMIT_BYTES),
        name="rest",
    )(x, yAT, kmem, vmem, *rest_params, *tail_params)


def _rope_tables(seq):
    rows = seq // GRID_W
    row = jnp.repeat(jnp.arange(rows, dtype=F32), GRID_W)
    col = jnp.tile(jnp.arange(GRID_W, dtype=F32), rows)
    n_freq = A_HEAD_DIM // 4
    inv = ROPE_THETA ** (-jnp.arange(n_freq, dtype=F32) / n_freq)
    ang = jnp.stack([row[:, None] * inv, col[:, None] * inv], axis=1)
    cos, sin = jnp.cos(ang), jnp.sin(ang)
    cos_hd = jnp.broadcast_to(cos[:, :, None, :], (seq, 2, 2, n_freq)).reshape(seq, A_HEAD_DIM)
    sin_hd = jnp.stack([-sin, sin], axis=2).reshape(seq, A_HEAD_DIM)
    reps = LANES // A_HEAD_DIM
    return jnp.tile(cos_hd, (1, reps)), jnp.tile(sin_hd, (1, reps))


def kernel(x, mem, norm_g, w_in, q_norm_g, k_norm_g, sg_ln_g, sg_ln_b, w_s, b_s,
           mem_norm_g, w_mem_kv, w_br, w_out, final_g):
    bsz, seq, _ = x.shape
    cos, sin = _rope_tables(seq)
    lane_head = jnp.arange(ONES_WIDTH) // A_HEAD_DIM
    ones = (lane_head[:, None] == lane_head[None, :]).astype(BF16)
    reps = LANES // A_HEAD_DIM
    q_scale = A_HEAD_DIM ** -0.5 * LOG2E

    col = jnp.arange(IN_WIDTH) - QKV_WIDTH
    halved = ((col >= 0) & (col < 512)) | ((col >= 1536) & (col < 2048)) | (col >= 2560)
    w_in_bf = (w_in * jnp.where(halved, 0.5, 1.0).astype(F32)).astype(BF16)
    g = norm_g.reshape(DEPTH, 1, D_MODEL)
    gq = (jnp.tile(q_norm_g, (1, reps)) * q_scale).reshape(DEPTH, 1, LANES)
    gk = jnp.tile(k_norm_g, (1, reps)).reshape(DEPTH, 1, LANES)
    lng = sg_ln_g.reshape(DEPTH, 1, B_WIDTH)
    lnb = sg_ln_b.reshape(DEPTH, 1, B_WIDTH)
    ws = w_s.astype(BF16)
    bs = jnp.broadcast_to(b_s[:, :, :, None], (DEPTH, B_GROUPS, CHUNK, B_GROUP_DIM))
    wbr = w_br.astype(BF16)
    wout = (w_out * 0.5).astype(BF16)
    fg = final_g.reshape(1, D_MODEL)
    kmem, vmem = _mem_kv(mem, mem_norm_g, w_mem_kv.astype(BF16))
    qkv_params = (g, w_in_bf, gq, gk, cos, sin, ones)
    assert len(qkv_params) == N_QKV_PARAMS
    rest_params = (g, w_in_bf, lng, lnb, ws, bs, wbr, wout)
    qT, k, vT = _qkv(x, 0, qkv_params)
    for l in range(DEPTH - 1):
        yAT = _attn(qT, k, vT)
        x, qT, k, vT = _rest(x, yAT, kmem, vmem, l, rest_params, qkv_params, final=False)
    yAT = _attn(qT, k, vT)
    return _rest(x, yAT, kmem, vmem, DEPTH - 1, rest_params, (fg,), final=True)
```

```python
import functools
import math

import jax
import jax.numpy as jnp
from jax import lax
from jax.experimental import pallas as pl
from jax.experimental.pallas import tpu as pltpu

D_MODEL = 1024
DEPTH = 4
GRID_W = 64
CHUNK = 128
ROPE_THETA = 10000.0
EPS = 1e-6
A_HEADS = 8
A_KV_HEADS = 2
A_HEAD_DIM = 64
A_WIDTH = A_HEADS * A_HEAD_DIM
A_KV_WIDTH = A_KV_HEADS * A_HEAD_DIM
B_GROUPS = 4
B_GROUP_DIM = 128
B_WIDTH = B_GROUPS * B_GROUP_DIM
M_HEADS = 4
M_HEAD_DIM = 128
M_WIDTH = M_HEADS * M_HEAD_DIM
MEM_LEN = 256
N_BRANCH = 3
QKV_WIDTH = A_WIDTH + 2 * A_KV_WIDTH
REST_WIDTH = 512 + 3 * 512 + 2 * 512 + N_BRANCH * D_MODEL
IN_WIDTH = QKV_WIDTH + REST_WIDTH

LANES = 128
LOG2E = math.log2(math.e)
VMEM_LIMIT_BYTES = 56 * 1024 * 1024

QKV_TILE = 512
ATTN_TILE = 512
ATTN_KV_CHUNK = 256
REST_TILE = 512

F32 = jnp.float32
BF16 = jnp.bfloat16


def _rms(x, g):
    return x * lax.rsqrt(jnp.mean(x * x, axis=-1, keepdims=True) + EPS) * g


def _dot(a, b):
    return jnp.dot(a, b, preferred_element_type=F32)


def _half_silu(hz):
    return hz + hz * jnp.tanh(hz)


def _const_spec(shape):
    nd = len(shape)
    return pl.BlockSpec(shape, lambda *_: (0,) * nd, pipeline_mode=pl.Buffered(1))


def _layer_spec(shape, layer):
    nd = len(shape)
    return pl.BlockSpec((None,) + shape, lambda *_: (layer,) + (0,) * nd,
                        pipeline_mode=pl.Buffered(1))


def _mem_kernel(mem_ref, g_ref, w_ref, k_ref, v_ref):
    n = _rms(mem_ref[...], g_ref[...]).astype(BF16)
    kv = _dot(n, w_ref[...])
    k_ref[...] = (kv[:, :M_WIDTH] * (M_HEAD_DIM ** -0.5 * LOG2E)).astype(BF16)
    v_ref[...] = kv[:, M_WIDTH:].astype(BF16)


def _mem_kv(mem, mem_norm_g, w_mem_kv):
    bsz = mem.shape[0]
    out = jax.ShapeDtypeStruct((DEPTH, bsz, MEM_LEN, M_WIDTH), BF16)
    return pl.pallas_call(
        _mem_kernel,
        grid=(DEPTH, bsz),
        in_specs=[
            pl.BlockSpec((None, MEM_LEN, D_MODEL), lambda l, b: (b, 0, 0)),
            pl.BlockSpec((None, 1, D_MODEL), lambda l, b: (l, 0, 0)),
            pl.BlockSpec((None, D_MODEL, 2 * M_WIDTH), lambda l, b: (l, 0, 0)),
        ],
        out_specs=[
            pl.BlockSpec((None, None, MEM_LEN, M_WIDTH), lambda l, b: (l, b, 0, 0)),
            pl.BlockSpec((None, None, MEM_LEN, M_WIDTH), lambda l, b: (l, b, 0, 0)),
        ],
        out_shape=[out, out],
        compiler_params=pltpu.CompilerParams(
            dimension_semantics=("arbitrary", "arbitrary"),
            vmem_limit_bytes=VMEM_LIMIT_BYTES),
        name="mem_kv",
    )(mem, mem_norm_g.reshape(DEPTH, 1, D_MODEL), w_mem_kv)


def _qkv_body(x, g_ref, wqvT_ref, wk_ref, gqT_ref, gk_ref, cosT_ref, sinT_ref,
              cos_ref, sin_ref, ones_ref, qT_ref, k_ref, vT_ref):
    tm = x.shape[0]
    h = _rms(x, g_ref[...]).astype(BF16)
    qvT = lax.dot_general(wqvT_ref[...], h, (((1,), (1,)), ((), ())),
                          preferred_element_type=F32)
    gq = jnp.tile(gqT_ref[...], (1, tm // LANES))
    cosT = cosT_ref[...]
    sinT = sinT_ref[...]
    zeros = jnp.zeros((A_HEAD_DIM, tm), BF16)
    half_w = A_HEAD_DIM // 4
    for hd in range(A_HEADS):
        t = qvT[hd * A_HEAD_DIM:(hd + 1) * A_HEAD_DIM, :]
        ms = jnp.mean(t * t, axis=0, keepdims=True)
        tn = t * lax.rsqrt(ms + EPS) * gq
        blocks = [tn[r:r + half_w, :] for r in range(0, A_HEAD_DIM, half_w)]
        partner = jnp.concatenate([blocks[i ^ 1] for i in range(len(blocks))], axis=0)
        piece = (tn * cosT + partner * sinT).astype(BF16)
        lo_rows = pl.ds(hd * LANES, A_HEAD_DIM)
        hi_rows = pl.ds(hd * LANES + A_HEAD_DIM, A_HEAD_DIM)
        if hd // (A_HEADS // A_KV_HEADS) == 0:
            qT_ref[lo_rows, :] = piece
            qT_ref[hi_rows, :] = zeros
        else:
            qT_ref[lo_rows, :] = zeros
            qT_ref[hi_rows, :] = piece
    vT_ref[...] = qvT[A_WIDTH:, :].astype(BF16)

    kk = _dot(h, wk_ref[...])
    ms = _dot((kk * kk).astype(BF16), ones_ref[...]) * (1.0 / A_HEAD_DIM)
    kn = kk * lax.rsqrt(ms + EPS) * gk_ref[...]
    lane = lax.broadcasted_iota(jnp.int32, (tm, LANES), 1)
    partner = jnp.where((lane & half_w) == 0,
                        pltpu.roll(kn, LANES - half_w, 1),
                        pltpu.roll(kn, half_w, 1))
    k_ref[...] = (kn * cos_ref[...] + partner * sin_ref[...]).astype(BF16)


N_QKV_PARAMS = 10


def _qkv_kernel(x_ref, *refs):
    _qkv_body(x_ref[...], *refs)


def _qkv_in_specs(tm, layer):
    return [
        _layer_spec((1, D_MODEL), layer),
        _layer_spec((A_WIDTH + A_KV_WIDTH, D_MODEL), layer),
        _layer_spec((D_MODEL, A_KV_WIDTH), layer),
        _layer_spec((A_HEAD_DIM, LANES), layer),
        _layer_spec((1, LANES), layer),
        pl.BlockSpec((A_HEAD_DIM, tm), lambda b, i: (0, i)),
        pl.BlockSpec((A_HEAD_DIM, tm), lambda b, i: (0, i)),
        pl.BlockSpec((tm, LANES), lambda b, i: (i, 0)),
        pl.BlockSpec((tm, LANES), lambda b, i: (i, 0)),
        _const_spec((LANES, LANES)),
    ]


def _qkv_out_specs(tm):
    return [
        pl.BlockSpec((None, A_HEADS * LANES, tm), lambda b, i: (b, 0, i)),
        pl.BlockSpec((None, tm, A_KV_WIDTH), lambda b, i: (b, i, 0)),
        pl.BlockSpec((None, A_KV_WIDTH, tm), lambda b, i: (b, 0, i)),
    ]


def _qkv_out_shapes(bsz, seq):
    return [
        jax.ShapeDtypeStruct((bsz, A_HEADS * LANES, seq), BF16),
        jax.ShapeDtypeStruct((bsz, seq, A_KV_WIDTH), BF16),
        jax.ShapeDtypeStruct((bsz, A_KV_WIDTH, seq), BF16),
    ]


def _qkv(x, layer, qkv_params):
    bsz, seq, _ = x.shape
    tm = QKV_TILE
    return pl.pallas_call(
        _qkv_kernel,
        grid=(bsz, seq // tm),
        in_specs=[pl.BlockSpec((None, tm, D_MODEL), lambda b, i: (b, i, 0))]
        + _qkv_in_specs(tm, layer),
        out_specs=_qkv_out_specs(tm),
        out_shape=_qkv_out_shapes(bsz, seq),
        compiler_params=pltpu.CompilerParams(
            dimension_semantics=("arbitrary", "arbitrary"),
            vmem_limit_bytes=VMEM_LIMIT_BYTES),
        name="qkv",
    )(x, *qkv_params)


def _attn_kernel(q_ref, k_ref, vT_ref, qn_ref, kn_ref, o_ref, s_even, s_odd, m_ref):
    seq, tq = s_even.shape
    n_chunks = seq // ATTN_KV_CHUNK
    grp = A_HEADS // A_KV_HEADS

    def rows8(a):
        return a.reshape(a.shape[0] // 8, 8, tq)

    def stage(q, k_src, s_qk, t_sm, s_sm, m_sm):
        m8 = None
        if t_sm is not None:
            v_rows = pl.ds(pl.multiple_of((t_sm // grp) * A_HEAD_DIM, A_HEAD_DIM), A_HEAD_DIM)
            acc = jnp.zeros((A_HEAD_DIM, tq), F32)
            l8 = jnp.zeros((8, tq), F32)
        for c in range(n_chunks):
            kv = slice(c * ATTN_KV_CHUNK, (c + 1) * ATTN_KV_CHUNK)
            sc = _dot(k_src[kv, :], q)
            s_qk[kv, :] = sc
            cm = jnp.max(rows8(sc), axis=0)
            m8 = cm if m8 is None else jnp.maximum(m8, cm)
            if t_sm is not None:
                p = jnp.exp2(s_sm[kv, :] - m_sm)
                l8 = l8 + jnp.sum(rows8(p), axis=0)
                acc = acc + _dot(vT_ref[v_rows, kv], p.astype(BF16))
        if t_sm is not None:
            l = jnp.sum(l8, axis=0, keepdims=True)
            o_rows = pl.ds(pl.multiple_of(t_sm * A_HEAD_DIM, A_HEAD_DIM), A_HEAD_DIM)
            o_ref[o_rows, :] = acc * (1.0 / l)
        return jnp.max(m8, axis=0, keepdims=True)

    @pl.when((pl.program_id(0) == 0) & (pl.program_id(1) == 0))
    def _():
        m_ref[...] = stage(q_ref[0], k_ref, s_even, None, None, None)

    def pair(u, m_even):
        m_odd = stage(q_ref[2 * u + 1], k_ref, s_odd, 2 * u, s_even, m_even)
        return stage(q_ref[2 * u + 2], k_ref, s_even, 2 * u + 1, s_odd, m_odd)

    m_even = lax.fori_loop(0, A_HEADS // 2 - 1, pair, m_ref[...])
    m_odd = stage(q_ref[A_HEADS - 1], k_ref, s_odd, A_HEADS - 2, s_even, m_even)
    m_ref[...] = stage(qn_ref[...], kn_ref, s_even, A_HEADS - 1, s_odd, m_odd)


def _attn(qT, k, vT):
    bsz, seq, _ = k.shape
    tq = ATTN_TILE
    n_i = seq // tq

    def next_step(b, i):
        nxt = jnp.minimum(b * n_i + i + 1, bsz * n_i - 1)
        return nxt // n_i, nxt % n_i

    def qn_map(b, i):
        nb, ni = next_step(b, i)
        return nb, 0, 0, ni

    def kn_map(b, i):
        return next_step(b, i)[0], 0, 0

    q4 = qT.reshape(bsz, A_HEADS, LANES, seq)
    return pl.pallas_call(
        _attn_kernel,
        grid=(bsz, n_i),
        in_specs=[
            pl.BlockSpec((None, A_HEADS, LANES, tq), lambda b, i: (b, 0, 0, i)),
            pl.BlockSpec((None, seq, A_KV_WIDTH), lambda b, i: (b, 0, 0)),
            pl.BlockSpec((None, A_KV_WIDTH, seq), lambda b, i: (b, 0, 0)),
            pl.BlockSpec((None, None, LANES, tq), qn_map),
            pl.BlockSpec((None, seq, A_KV_WIDTH), kn_map),
        ],
        out_specs=pl.BlockSpec((None, A_WIDTH, tq), lambda b, i: (b, 0, i)),
        out_shape=jax.ShapeDtypeStruct((bsz, A_WIDTH, seq), F32),
        scratch_shapes=[pltpu.VMEM((seq, tq), F32), pltpu.VMEM((seq, tq), F32),
                        pltpu.VMEM((1, tq), F32)],
        compiler_params=pltpu.CompilerParams(
            dimension_semantics=("arbitrary", "arbitrary"),
            vmem_limit_bytes=VMEM_LIMIT_BYTES),
        name="attn",
    )(q4, k, vT, q4, k)


def _rest_kernel(x_ref, yAT_ref, kmem_ref, vmem_ref, g_ref, w_ref, lng_ref, lnb_ref,
                 ws_ref, bs_ref, wbr_ref, wout_ref, *refs, final):
    if final:
        fg_ref, o_ref = refs
    else:
        qkv_in, o_ref, qkv_out = refs[:N_QKV_PARAMS], refs[N_QKV_PARAMS], refs[N_QKV_PARAMS + 1:]
    tq = x_ref.shape[0]
    x = x_ref[...]
    h = _rms(x, g_ref[...]).astype(BF16)
    proj = _dot(h, w_ref[:, QKV_WIDTH:])
    hzA = proj[:, 0:512]
    uB = proj[:, 512:1024]
    vB = proj[:, 1024:1536]
    hzB = proj[:, 1536:2048]
    qM = proj[:, 2048:2560]
    hzM = proj[:, 2560:3072]

    yA = yAT_ref[...].T * _half_silu(hzA)

    mu = jnp.mean(vB, axis=-1, keepdims=True)
    vc = vB - mu
    vn = vc * lax.rsqrt(jnp.mean(vc * vc, axis=-1, keepdims=True) + EPS)
    vn = (vn * lng_ref[...] + lnb_ref[...]).astype(BF16)
    n_chunk = tq // CHUNK
    per_group = []
    for g in range(B_GROUPS):
        gcols = slice(g * B_GROUP_DIM, (g + 1) * B_GROUP_DIM)
        wide = jnp.concatenate([vn[c * CHUNK:(c + 1) * CHUNK, gcols] for c in range(n_chunk)],
                               axis=1)
        per_group.append(_dot(ws_ref[g], wide))
    bias = jnp.concatenate([bs_ref[g] for g in range(B_GROUPS)], axis=1)
    mixed = jnp.concatenate(
        [jnp.concatenate([per_group[g][:, c * B_GROUP_DIM:(c + 1) * B_GROUP_DIM]
                          for g in range(B_GROUPS)], axis=1) + bias
         for c in range(n_chunk)], axis=0)
    yB = uB * mixed * _half_silu(hzB)

    heads = []
    for hh in range(M_HEADS):
        sl = slice(hh * M_HEAD_DIM, (hh + 1) * M_HEAD_DIM)
        s = lax.dot_general(qM[:, sl].astype(BF16), kmem_ref[:, sl],
                            (((1,), (1,)), ((), ())), preferred_element_type=F32)
        p = jnp.exp2(s - jnp.max(s, axis=-1, keepdims=True))
        l = jnp.sum(p, axis=-1, keepdims=True)
        heads.append(_dot(p.astype(BF16), vmem_ref[:, sl]) * (1.0 / l))
    yM = jnp.concatenate(heads, axis=1) * _half_silu(hzM)

    merged2 = None
    for n, y in enumerate((yA, yB, yM)):
        up = _dot(y.astype(BF16), wbr_ref[n])
        gate2 = 1.0 + jnp.tanh(proj[:, 3072 + n * D_MODEL:3072 + (n + 1) * D_MODEL])
        merged2 = gate2 * up if merged2 is None else merged2 + gate2 * up
    out = x + _dot(merged2.astype(BF16), wout_ref[...])
    if final:
        o_ref[...] = _rms(out, fg_ref[...])
    else:
        o_ref[...] = out
        _qkv_body(out, *qkv_in, *qkv_out)


def _rest(x, yAT, kmem, vmem, layer, rest_params, tail_params, final):
    bsz, seq, _ = x.shape
    tq = REST_TILE
    x_spec = pl.BlockSpec((None, tq, D_MODEL), lambda b, i: (b, i, 0))
    x_shape = jax.ShapeDtypeStruct((bsz, seq, D_MODEL), F32)
    if final:
        tail_specs = [_const_spec((1, D_MODEL))]
        out_specs, out_shape = x_spec, x_shape
    else:
        tail_specs = _qkv_in_specs(tq, layer + 1)
        out_specs = [x_spec] + _qkv_out_specs(tq)
        out_shape = [x_shape] + _qkv_out_shapes(bsz, seq)
    return pl.pallas_call(
        functools.partial(_rest_kernel, final=final),
        grid=(bsz, seq // tq),
        in_specs=[
            pl.BlockSpec((None, tq, D_MODEL), lambda b, i: (b, i, 0)),
            pl.BlockSpec((None, A_WIDTH, tq), lambda b, i: (b, 0, i)),
            pl.BlockSpec((None, None, MEM_LEN, M_WIDTH), lambda b, i: (layer, b, 0, 0)),
            pl.BlockSpec((None, None, MEM_LEN, M_WIDTH), lambda b, i: (layer, b, 0, 0)),
            _layer_spec((1, D_MODEL), layer),
            _layer_spec((D_MODEL, IN_WIDTH), layer),
            _layer_spec((1, B_WIDTH), layer),
            _layer_spec((1, B_WIDTH), layer),
            _layer_spec((B_GROUPS, CHUNK, CHUNK), layer),
            _layer_spec((B_GROUPS, CHUNK, B_GROUP_DIM), layer),
            _layer_spec((N_BRANCH, 512, D_MODEL), layer),
            _layer_spec((D_MODEL, D_MODEL), layer),
        ] + tail_specs,
        out_specs=out_specs,
        out_shape=out_shape,
        compiler_params=pltpu.CompilerParams(
            dimension_semantics=("arbitrary", "arbitrary"),
            vmem_limit_bytes=VMEM_LIMIT_BYTES),
        name="rest",
    )(x, yAT, kmem, vmem, *rest_params, *tail_params)


def _rope_tables(seq):
    rows = seq // GRID_W
    row = jnp.repeat(jnp.arange(rows, dtype=F32), GRID_W)
    col = jnp.tile(jnp.arange(GRID_W, dtype=F32), rows)
    n_freq = A_HEAD_DIM // 4
    inv = ROPE_THETA ** (-jnp.arange(n_freq, dtype=F32) / n_freq)
    ang = jnp.stack([row[:, None] * inv, col[:, None] * inv], axis=1)
    cos, sin = jnp.cos(ang), jnp.sin(ang)
    cos_hd = jnp.broadcast_to(cos[:, :, None, :], (seq, 2, 2, n_freq)).reshape(seq, A_HEAD_DIM)
    sin_hd = jnp.stack([-sin, sin], axis=2).reshape(seq, A_HEAD_DIM)
    reps = LANES // A_HEAD_DIM
    return jnp.tile(cos_hd, (1, reps)), jnp.tile(sin_hd, (1, reps)), cos_hd.T, sin_hd.T


def kernel(x, mem, norm_g, w_in, q_norm_g, k_norm_g, sg_ln_g, sg_ln_b, w_s, b_s,
           mem_norm_g, w_mem_kv, w_br, w_out, final_g):
    bsz, seq, _ = x.shape
    cos, sin, cosT, sinT = _rope_tables(seq)
    lane_head = jnp.arange(LANES) // A_HEAD_DIM
    ones = (lane_head[:, None] == lane_head[None, :]).astype(BF16)
    reps = LANES // A_HEAD_DIM
    q_scale = A_HEAD_DIM ** -0.5 * LOG2E

    col = jnp.arange(IN_WIDTH) - QKV_WIDTH
    halved = ((col >= 0) & (col < 512)) | ((col >= 1536) & (col < 2048)) | (col >= 2560)
    w_in_bf = (w_in * jnp.where(halved, 0.5, 1.0).astype(F32)).astype(BF16)
    g = norm_g.reshape(DEPTH, 1, D_MODEL)
    gqT = jnp.broadcast_to((q_norm_g * q_scale)[:, :, None], (DEPTH, A_HEAD_DIM, LANES))
    w_qvT = jnp.concatenate([w_in[:, :, :A_WIDTH], w_in[:, :, A_WIDTH + A_KV_WIDTH:QKV_WIDTH]],
                            axis=2).transpose(0, 2, 1).astype(BF16)
    w_k = w_in[:, :, A_WIDTH:A_WIDTH + A_KV_WIDTH].astype(BF16)
    gk = jnp.tile(k_norm_g, (1, reps)).reshape(DEPTH, 1, LANES)
    lng = sg_ln_g.reshape(DEPTH, 1, B_WIDTH)
    lnb = sg_ln_b.reshape(DEPTH, 1, B_WIDTH)
    ws = w_s.astype(BF16)
    bs = jnp.broadcast_to(b_s[:, :, :, None], (DEPTH, B_GROUPS, CHUNK, B_GROUP_DIM))
    wbr = w_br.astype(BF16)
    wout = (w_out * 0.5).astype(BF16)
    fg = final_g.reshape(1, D_MODEL)
    kmem, vmem = _mem_kv(mem, mem_norm_g, w_mem_kv.astype(BF16))
    qkv_params = (g, w_qvT, w_k, gqT, gk, cosT, sinT, cos, sin, ones)
    assert len(qkv_params) == N_QKV_PARAMS
    rest_params = (g, w_in_bf, lng, lnb, ws, bs, wbr, wout)
    qT, k, vT = _qkv(x, 0, qkv_params)
    for l in range(DEPTH - 1):
        yAT = _attn(qT, k, vT)
        x, qT, k, vT = _rest(x, yAT, kmem, vmem, l, rest_params, qkv_params, final=False)
    yAT = _attn(qT, k, vT)
    return _rest(x, yAT, kmem, vmem, DEPTH - 1, rest_params, (fg,), final=True)
```

```python
import functools
import math

import jax
import jax.numpy as jnp
from jax import lax
from jax.experimental import pallas as pl
from jax.experimental.pallas import tpu as pltpu

D_MODEL = 1024
DEPTH = 4
GRID_W = 64
CHUNK = 128
ROPE_THETA = 10000.0
EPS = 1e-6
A_HEADS = 8
A_KV_HEADS = 2
A_HEAD_DIM = 64
A_WIDTH = A_HEADS * A_HEAD_DIM
A_KV_WIDTH = A_KV_HEADS * A_HEAD_DIM
B_GROUPS = 4
B_GROUP_DIM = 128
B_WIDTH = B_GROUPS * B_GROUP_DIM
M_HEADS = 4
M_HEAD_DIM = 128
M_WIDTH = M_HEADS * M_HEAD_DIM
MEM_LEN = 256
N_BRANCH = 3
QKV_WIDTH = A_WIDTH + 2 * A_KV_WIDTH
BRANCH_WIDTH = 512
COL_ZA, COL_UB, COL_VB, COL_ZB, COL_QM, COL_ZM, COL_GATE = (i * BRANCH_WIDTH for i in range(7))
REST_WIDTH = COL_GATE + N_BRANCH * D_MODEL
IN_WIDTH = QKV_WIDTH + REST_WIDTH

LANES = 128
LOG2E = math.log2(math.e)
VMEM_LIMIT_BYTES = 56 * 1024 * 1024

QKV_TILE = 512
ATTN_TILE = 512
ATTN_KV_CHUNK = 256
REST_TILE = 512

F32 = jnp.float32
BF16 = jnp.bfloat16


def _rms(x, g):
    return x * lax.rsqrt(jnp.mean(x * x, axis=-1, keepdims=True) + EPS) * g


def _dot(a, b):
    return jnp.dot(a, b, preferred_element_type=F32)


def _half_silu(hz):
    return hz + hz * jnp.tanh(hz)


def _const_spec(shape):
    nd = len(shape)
    return pl.BlockSpec(shape, lambda *_: (0,) * nd, pipeline_mode=pl.Buffered(1))


def _layer_spec(shape, layer):
    nd = len(shape)
    return pl.BlockSpec((None,) + shape, lambda *_: (layer,) + (0,) * nd,
                        pipeline_mode=pl.Buffered(1))


def _mem_kernel(mem_ref, g_ref, w_ref, k_ref, v_ref):
    n = _rms(mem_ref[...], g_ref[...]).astype(BF16)
    kv = _dot(n, w_ref[...])
    k_ref[...] = (kv[:, :M_WIDTH] * (M_HEAD_DIM ** -0.5 * LOG2E)).astype(BF16)
    v_ref[...] = kv[:, M_WIDTH:].astype(BF16)


def _mem_kv(mem, mem_norm_g, w_mem_kv):
    bsz = mem.shape[0]
    out = jax.ShapeDtypeStruct((DEPTH, bsz, MEM_LEN, M_WIDTH), BF16)
    return pl.pallas_call(
        _mem_kernel,
        grid=(DEPTH, bsz),
        in_specs=[
            pl.BlockSpec((None, MEM_LEN, D_MODEL), lambda l, b: (b, 0, 0)),
            pl.BlockSpec((None, 1, D_MODEL), lambda l, b: (l, 0, 0)),
            pl.BlockSpec((None, D_MODEL, 2 * M_WIDTH), lambda l, b: (l, 0, 0)),
        ],
        out_specs=[
            pl.BlockSpec((None, None, MEM_LEN, M_WIDTH), lambda l, b: (l, b, 0, 0)),
            pl.BlockSpec((None, None, MEM_LEN, M_WIDTH), lambda l, b: (l, b, 0, 0)),
        ],
        out_shape=[out, out],
        compiler_params=pltpu.CompilerParams(
            dimension_semantics=("arbitrary", "arbitrary"),
            vmem_limit_bytes=VMEM_LIMIT_BYTES),
        name="mem_kv",
    )(mem, mem_norm_g.reshape(DEPTH, 1, D_MODEL), w_mem_kv)


def _qkv_body(x, g_ref, wqvT_ref, wk_ref, gqT_ref, gk_ref, cosT_ref, sinT_ref,
              cos_ref, sin_ref, ones_ref, qT_ref, k_ref, vT_ref):
    tm = x.shape[0]
    h = _rms(x, g_ref[...]).astype(BF16)
    qvT = lax.dot_general(wqvT_ref[...], h, (((1,), (1,)), ((), ())),
                          preferred_element_type=F32)
    gq = jnp.tile(gqT_ref[...], (1, tm // LANES))
    cosT = cosT_ref[...]
    sinT = sinT_ref[...]
    zeros = jnp.zeros((A_HEAD_DIM, tm), BF16)
    half_w = A_HEAD_DIM // 4
    for hd in range(A_HEADS):
        t = qvT[hd * A_HEAD_DIM:(hd + 1) * A_HEAD_DIM, :]
        ms = jnp.mean(t * t, axis=0, keepdims=True)
        tn = t * lax.rsqrt(ms + EPS) * gq
        blocks = [tn[r:r + half_w, :] for r in range(0, A_HEAD_DIM, half_w)]
        partner = jnp.concatenate([blocks[i ^ 1] for i in range(len(blocks))], axis=0)
        piece = (tn * cosT + partner * sinT).astype(BF16)
        lo_rows = pl.ds(hd * LANES, A_HEAD_DIM)
        hi_rows = pl.ds(hd * LANES + A_HEAD_DIM, A_HEAD_DIM)
        if hd // (A_HEADS // A_KV_HEADS) == 0:
            qT_ref[lo_rows, :] = piece
            qT_ref[hi_rows, :] = zeros
        else:
            qT_ref[lo_rows, :] = zeros
            qT_ref[hi_rows, :] = piece
    vT_ref[...] = qvT[A_WIDTH:, :].astype(BF16)

    kk = _dot(h, wk_ref[...])
    ms = _dot((kk * kk).astype(BF16), ones_ref[...]) * (1.0 / A_HEAD_DIM)
    kn = kk * lax.rsqrt(ms + EPS) * gk_ref[...]
    lane = lax.broadcasted_iota(jnp.int32, (tm, LANES), 1)
    partner = jnp.where((lane & half_w) == 0,
                        pltpu.roll(kn, LANES - half_w, 1),
                        pltpu.roll(kn, half_w, 1))
    k_ref[...] = (kn * cos_ref[...] + partner * sin_ref[...]).astype(BF16)


N_QKV_PARAMS = 10


def _qkv_kernel(x_ref, *refs):
    _qkv_body(x_ref[...], *refs)


def _qkv_in_specs(tm, layer):
    return [
        _layer_spec((1, D_MODEL), layer),
        _layer_spec((A_WIDTH + A_KV_WIDTH, D_MODEL), layer),
        _layer_spec((D_MODEL, A_KV_WIDTH), layer),
        _layer_spec((A_HEAD_DIM, LANES), layer),
        _layer_spec((1, LANES), layer),
        pl.BlockSpec((A_HEAD_DIM, tm), lambda b, i: (0, i)),
        pl.BlockSpec((A_HEAD_DIM, tm), lambda b, i: (0, i)),
        pl.BlockSpec((tm, LANES), lambda b, i: (i, 0)),
        pl.BlockSpec((tm, LANES), lambda b, i: (i, 0)),
        _const_spec((LANES, LANES)),
    ]


def _qkv_out_specs(tm):
    return [
        pl.BlockSpec((None, A_HEADS * LANES, tm), lambda b, i: (b, 0, i)),
        pl.BlockSpec((None, tm, A_KV_WIDTH), lambda b, i: (b, i, 0)),
        pl.BlockSpec((None, A_KV_WIDTH, tm), lambda b, i: (b, 0, i)),
    ]


def _qkv_out_shapes(bsz, seq):
    return [
        jax.ShapeDtypeStruct((bsz, A_HEADS * LANES, seq), BF16),
        jax.ShapeDtypeStruct((bsz, seq, A_KV_WIDTH), BF16),
        jax.ShapeDtypeStruct((bsz, A_KV_WIDTH, seq), BF16),
    ]


def _qkv(x, layer, qkv_params):
    bsz, seq, _ = x.shape
    tm = QKV_TILE
    return pl.pallas_call(
        _qkv_kernel,
        grid=(bsz, seq // tm),
        in_specs=[pl.BlockSpec((None, tm, D_MODEL), lambda b, i: (b, i, 0))]
        + _qkv_in_specs(tm, layer),
        out_specs=_qkv_out_specs(tm),
        out_shape=_qkv_out_shapes(bsz, seq),
        compiler_params=pltpu.CompilerParams(
            dimension_semantics=("arbitrary", "arbitrary"),
            vmem_limit_bytes=VMEM_LIMIT_BYTES),
        name="qkv",
    )(x, *qkv_params)


def _attn_kernel(q_ref, k_ref, vT_ref, qn_ref, kn_ref, o_ref, s_even, s_odd, m_ref):
    seq, tq = s_even.shape
    n_chunks = seq // ATTN_KV_CHUNK
    grp = A_HEADS // A_KV_HEADS

    def rows8(a):
        return a.reshape(a.shape[0] // 8, 8, tq)

    def stage(q, k_src, s_qk, t_sm, s_sm, m_sm):
        m8 = None
        if t_sm is not None:
            v_rows = pl.ds(pl.multiple_of((t_sm // grp) * A_HEAD_DIM, A_HEAD_DIM), A_HEAD_DIM)
            acc = jnp.zeros((A_HEAD_DIM, tq), F32)
            l8 = jnp.zeros((8, tq), F32)
        for c in range(n_chunks):
            kv = slice(c * ATTN_KV_CHUNK, (c + 1) * ATTN_KV_CHUNK)
            sc = _dot(k_src[kv, :], q)
            s_qk[kv, :] = sc
            cm = jnp.max(rows8(sc), axis=0)
            m8 = cm if m8 is None else jnp.maximum(m8, cm)
            if t_sm is not None:
                p = jnp.exp2(s_sm[kv, :] - m_sm)
                l8 = l8 + jnp.sum(rows8(p), axis=0)
                acc = acc + _dot(vT_ref[v_rows, kv], p.astype(BF16))
        if t_sm is not None:
            l = jnp.sum(l8, axis=0, keepdims=True)
            o_rows = pl.ds(pl.multiple_of(t_sm * A_HEAD_DIM, A_HEAD_DIM), A_HEAD_DIM)
            o_ref[o_rows, :] = acc * (1.0 / l)
        return jnp.max(m8, axis=0, keepdims=True)

    @pl.when((pl.program_id(0) == 0) & (pl.program_id(1) == 0))
    def _():
        m_ref[...] = stage(q_ref[0], k_ref, s_even, None, None, None)

    def pair(u, m_even):
        m_odd = stage(q_ref[2 * u + 1], k_ref, s_odd, 2 * u, s_even, m_even)
        return stage(q_ref[2 * u + 2], k_ref, s_even, 2 * u + 1, s_odd, m_odd)

    m_even = lax.fori_loop(0, A_HEADS // 2 - 1, pair, m_ref[...])
    m_odd = stage(q_ref[A_HEADS - 1], k_ref, s_odd, A_HEADS - 2, s_even, m_even)
    m_ref[...] = stage(qn_ref[...], kn_ref, s_even, A_HEADS - 1, s_odd, m_odd)


def _attn(qT, k, vT):
    bsz, seq, _ = k.shape
    tq = ATTN_TILE
    n_i = seq // tq

    def next_step(b, i):
        nxt = jnp.minimum(b * n_i + i + 1, bsz * n_i - 1)
        return nxt // n_i, nxt % n_i

    def qn_map(b, i):
        nb, ni = next_step(b, i)
        return nb, 0, 0, ni

    def kn_map(b, i):
        return next_step(b, i)[0], 0, 0

    q4 = qT.reshape(bsz, A_HEADS, LANES, seq)
    return pl.pallas_call(
        _attn_kernel,
        grid=(bsz, n_i),
        in_specs=[
            pl.BlockSpec((None, A_HEADS, LANES, tq), lambda b, i: (b, 0, 0, i)),
            pl.BlockSpec((None, seq, A_KV_WIDTH), lambda b, i: (b, 0, 0)),
            pl.BlockSpec((None, A_KV_WIDTH, seq), lambda b, i: (b, 0, 0)),
            pl.BlockSpec((None, None, LANES, tq), qn_map),
            pl.BlockSpec((None, seq, A_KV_WIDTH), kn_map),
        ],
        out_specs=pl.BlockSpec((None, A_WIDTH, tq), lambda b, i: (b, 0, i)),
        out_shape=jax.ShapeDtypeStruct((bsz, A_WIDTH, seq), F32),
        scratch_shapes=[pltpu.VMEM((seq, tq), F32), pltpu.VMEM((seq, tq), F32),
                        pltpu.VMEM((1, tq), F32)],
        compiler_params=pltpu.CompilerParams(
            dimension_semantics=("arbitrary", "arbitrary"),
            vmem_limit_bytes=VMEM_LIMIT_BYTES),
        name="attn",
    )(q4, k, vT, q4, k)


def _rest_kernel(x_ref, yAT_ref, kmem_ref, vmem_ref, g_ref, w_ref, lng_ref, lnb_ref,
                 ws_ref, bs_ref, wbr_ref, wout_ref, *refs, final):
    if final:
        fg_ref, o_ref = refs
    else:
        qkv_in, o_ref, qkv_out = refs[:N_QKV_PARAMS], refs[N_QKV_PARAMS], refs[N_QKV_PARAMS + 1:]
    tq = x_ref.shape[0]
    x = x_ref[...]
    h = _rms(x, g_ref[...]).astype(BF16)
    proj = _dot(h, w_ref[:, QKV_WIDTH:])
    hzA = proj[:, COL_ZA:COL_UB]
    uB = proj[:, COL_UB:COL_VB]
    vB = proj[:, COL_VB:COL_ZB]
    hzB = proj[:, COL_ZB:COL_QM]
    qM = proj[:, COL_QM:COL_ZM]
    hzM = proj[:, COL_ZM:COL_GATE]

    yA = yAT_ref[...].T * _half_silu(hzA)

    mu = jnp.mean(vB, axis=-1, keepdims=True)
    vc = vB - mu
    vn = vc * lax.rsqrt(jnp.mean(vc * vc, axis=-1, keepdims=True) + EPS)
    vn = (vn * lng_ref[...] + lnb_ref[...]).astype(BF16)
    n_chunk = tq // CHUNK
    per_group = []
    for g in range(B_GROUPS):
        gcols = slice(g * B_GROUP_DIM, (g + 1) * B_GROUP_DIM)
        wide = jnp.concatenate([vn[c * CHUNK:(c + 1) * CHUNK, gcols] for c in range(n_chunk)],
                               axis=1)
        per_group.append(_dot(ws_ref[g], wide))
    bias = jnp.concatenate([bs_ref[g] for g in range(B_GROUPS)], axis=1)
    mixed = jnp.concatenate(
        [jnp.concatenate([per_group[g][:, c * B_GROUP_DIM:(c + 1) * B_GROUP_DIM]
                          for g in range(B_GROUPS)], axis=1) + bias
         for c in range(n_chunk)], axis=0)
    yB = uB * mixed * _half_silu(hzB)

    heads = []
    for hh in range(M_HEADS):
        sl = slice(hh * M_HEAD_DIM, (hh + 1) * M_HEAD_DIM)
        s = lax.dot_general(qM[:, sl].astype(BF16), kmem_ref[:, sl],
                            (((1,), (1,)), ((), ())), preferred_element_type=F32)
        p = jnp.exp2(s - jnp.max(s, axis=-1, keepdims=True))
        l = jnp.sum(p, axis=-1, keepdims=True)
        heads.append(_dot(p.astype(BF16), vmem_ref[:, sl]) * (1.0 / l))
    yM = jnp.concatenate(heads, axis=1) * _half_silu(hzM)

    merged2 = None
    for n, y in enumerate((yA, yB, yM)):
        up = _dot(y.astype(BF16), wbr_ref[n])
        gate2 = 1.0 + jnp.tanh(proj[:, COL_GATE + n * D_MODEL:COL_GATE + (n + 1) * D_MODEL])
        merged2 = gate2 * up if merged2 is None else merged2 + gate2 * up
    out = x + _dot(merged2.astype(BF16), wout_ref[...])
    if final:
        o_ref[...] = _rms(out, fg_ref[...])
    else:
        o_ref[...] = out
        _qkv_body(out, *qkv_in, *qkv_out)


def _rest(x, yAT, kmem, vmem, layer, rest_params, tail_params, final):
    bsz, seq, _ = x.shape
    tq = REST_TILE
    x_spec = pl.BlockSpec((None, tq, D_MODEL), lambda b, i: (b, i, 0))
    x_shape = jax.ShapeDtypeStruct((bsz, seq, D_MODEL), F32)
    if final:
        tail_specs = [_const_spec((1, D_MODEL))]
        out_specs, out_shape = x_spec, x_shape
    else:
        tail_specs = _qkv_in_specs(tq, layer + 1)
        out_specs = [x_spec] + _qkv_out_specs(tq)
        out_shape = [x_shape] + _qkv_out_shapes(bsz, seq)
    return pl.pallas_call(
        functools.partial(_rest_kernel, final=final),
        grid=(bsz, seq // tq),
        in_specs=[
            pl.BlockSpec((None, tq, D_MODEL), lambda b, i: (b, i, 0)),
            pl.BlockSpec((None, A_WIDTH, tq), lambda b, i: (b, 0, i)),
            pl.BlockSpec((None, None, MEM_LEN, M_WIDTH), lambda b, i: (layer, b, 0, 0)),
            pl.BlockSpec((None, None, MEM_LEN, M_WIDTH), lambda b, i: (layer, b, 0, 0)),
            _layer_spec((1, D_MODEL), layer),
            _layer_spec((D_MODEL, IN_WIDTH), layer),
            _layer_spec((1, B_WIDTH), layer),
            _layer_spec((1, B_WIDTH), layer),
            _layer_spec((B_GROUPS, CHUNK, CHUNK), layer),
            _layer_spec((B_GROUPS, CHUNK, B_GROUP_DIM), layer),
            _layer_spec((N_BRANCH, BRANCH_WIDTH, D_MODEL), layer),
            _layer_spec((D_MODEL, D_MODEL), layer),
        ] + tail_specs,
        out_specs=out_specs,
        out_shape=out_shape,
        compiler_params=pltpu.CompilerParams(
            dimension_semantics=("arbitrary", "arbitrary"),
            vmem_limit_bytes=VMEM_LIMIT_BYTES),
        name="rest",
    )(x, yAT, kmem, vmem, *rest_params, *tail_params)


def _rope_tables(seq):
    rows = seq // GRID_W
    row = jnp.repeat(jnp.arange(rows, dtype=F32), GRID_W)
    col = jnp.tile(jnp.arange(GRID_W, dtype=F32), rows)
    n_freq = A_HEAD_DIM // 4
    inv = ROPE_THETA ** (-jnp.arange(n_freq, dtype=F32) / n_freq)
    ang = jnp.stack([row[:, None] * inv, col[:, None] * inv], axis=1)
    cos, sin = jnp.cos(ang), jnp.sin(ang)
    cos_hd = jnp.broadcast_to(cos[:, :, None, :], (seq, 2, 2, n_freq)).reshape(seq, A_HEAD_DIM)
    sin_hd = jnp.stack([-sin, sin], axis=2).reshape(seq, A_HEAD_DIM)
    reps = LANES // A_HEAD_DIM
    return jnp.tile(cos_hd, (1, reps)), jnp.tile(sin_hd, (1, reps)), cos_hd.T, sin_hd.T


def kernel(x, mem, norm_g, w_in, q_norm_g, k_norm_g, sg_ln_g, sg_ln_b, w_s, b_s,
           mem_norm_g, w_mem_kv, w_br, w_out, final_g):
    bsz, seq, _ = x.shape
    cos, sin, cosT, sinT = _rope_tables(seq)
    lane_head = jnp.arange(LANES) // A_HEAD_DIM
    ones = (lane_head[:, None] == lane_head[None, :]).astype(BF16)
    reps = LANES // A_HEAD_DIM
    q_scale = A_HEAD_DIM ** -0.5 * LOG2E

    col = jnp.arange(IN_WIDTH) - QKV_WIDTH
    halved = (((col >= COL_ZA) & (col < COL_UB)) | ((col >= COL_ZB) & (col < COL_QM))
              | (col >= COL_ZM))
    w_in_bf = (w_in * jnp.where(halved, 0.5, 1.0).astype(F32)).astype(BF16)
    g = norm_g.reshape(DEPTH, 1, D_MODEL)
    gqT = jnp.broadcast_to((q_norm_g * q_scale)[:, :, None], (DEPTH, A_HEAD_DIM, LANES))
    w_qvT = jnp.concatenate([w_in_bf[:, :, :A_WIDTH],
                             w_in_bf[:, :, A_WIDTH + A_KV_WIDTH:QKV_WIDTH]], axis=2).transpose(0, 2, 1)
    w_k = w_in_bf[:, :, A_WIDTH:A_WIDTH + A_KV_WIDTH]
    gk = jnp.tile(k_norm_g, (1, reps)).reshape(DEPTH, 1, LANES)
    lng = sg_ln_g.reshape(DEPTH, 1, B_WIDTH)
    lnb = sg_ln_b.reshape(DEPTH, 1, B_WIDTH)
    ws = w_s.astype(BF16)
    bs = jnp.broadcast_to(b_s[:, :, :, None], (DEPTH, B_GROUPS, CHUNK, B_GROUP_DIM))
    wbr = w_br.astype(BF16)
    wout = (w_out * 0.5).astype(BF16)
    fg = final_g.reshape(1, D_MODEL)
    kmem, vmem = _mem_kv(mem, mem_norm_g, w_mem_kv.astype(BF16))
    qkv_params = (g, w_qvT, w_k, gqT, gk, cosT, sinT, cos, sin, ones)
    assert len(qkv_params) == N_QKV_PARAMS
    rest_params = (g, w_in_bf, lng, lnb, ws, bs, wbr, wout)
    qT, k, vT = _qkv(x, 0, qkv_params)
    for l in range(DEPTH - 1):
        yAT = _attn(qT, k, vT)
        x, qT, k, vT = _rest(x, yAT, kmem, vmem, l, rest_params, qkv_params, final=False)
    yAT = _attn(qT, k, vT)
    return _rest(x, yAT, kmem, vmem, DEPTH - 1, rest_params, (fg,), final=True)
```

```python
import functools
import math

import jax
import jax.numpy as jnp
from jax import lax
from jax.experimental import pallas as pl
from jax.experimental.pallas import tpu as pltpu

D_MODEL = 1024
DEPTH = 4
GRID_W = 64
CHUNK = 128
ROPE_THETA = 10000.0
EPS = 1e-6
A_HEADS = 8
A_KV_HEADS = 2
A_HEAD_DIM = 64
A_WIDTH = A_HEADS * A_HEAD_DIM
A_KV_WIDTH = A_KV_HEADS * A_HEAD_DIM
B_GROUPS = 4
B_GROUP_DIM = 128
B_WIDTH = B_GROUPS * B_GROUP_DIM
M_HEADS = 4
M_HEAD_DIM = 128
M_WIDTH = M_HEADS * M_HEAD_DIM
MEM_LEN = 256
N_BRANCH = 3
QKV_WIDTH = A_WIDTH + 2 * A_KV_WIDTH
BRANCH_WIDTH = 512
COL_ZA, COL_UB, COL_VB, COL_ZB, COL_QM, COL_ZM, COL_GATE = (i * BRANCH_WIDTH for i in range(7))
REST_WIDTH = COL_GATE + N_BRANCH * D_MODEL
IN_WIDTH = QKV_WIDTH + REST_WIDTH

LANES = 128
LOG2E = math.log2(math.e)
VMEM_LIMIT_BYTES = 56 * 1024 * 1024

QKV_TILE = 512
ATTN_TILE = 512
ATTN_KV_CHUNK = 256
REST_TILE = 512

F32 = jnp.float32
BF16 = jnp.bfloat16


def _rms(x, g):
    return x * lax.rsqrt(jnp.mean(x * x, axis=-1, keepdims=True) + EPS) * g


def _dot(a, b):
    return jnp.dot(a, b, preferred_element_type=F32)


def _half_silu(hz):
    return hz + hz * jnp.tanh(hz)


def _const_spec(shape):
    nd = len(shape)
    return pl.BlockSpec(shape, lambda *_: (0,) * nd, pipeline_mode=pl.Buffered(1))


def _layer_spec(shape, layer):
    nd = len(shape)
    return pl.BlockSpec((None,) + shape, lambda *_: (layer,) + (0,) * nd,
                        pipeline_mode=pl.Buffered(1))


def _mem_kernel(mem_ref, g_ref, w_ref, k_ref, v_ref):
    n = _rms(mem_ref[...], g_ref[...]).astype(BF16)
    kv = _dot(n, w_ref[...])
    k_ref[...] = (kv[:, :M_WIDTH] * (M_HEAD_DIM ** -0.5 * LOG2E)).astype(BF16)
    v_ref[...] = kv[:, M_WIDTH:].astype(BF16)


def _mem_kv(mem, mem_norm_g, w_mem_kv):
    bsz = mem.shape[0]
    out = jax.ShapeDtypeStruct((DEPTH, bsz, MEM_LEN, M_WIDTH), BF16)
    return pl.pallas_call(
        _mem_kernel,
        grid=(DEPTH, bsz),
        in_specs=[
            pl.BlockSpec((None, MEM_LEN, D_MODEL), lambda l, b: (b, 0, 0)),
            pl.BlockSpec((None, 1, D_MODEL), lambda l, b: (l, 0, 0)),
            pl.BlockSpec((None, D_MODEL, 2 * M_WIDTH), lambda l, b: (l, 0, 0)),
        ],
        out_specs=[
            pl.BlockSpec((None, None, MEM_LEN, M_WIDTH), lambda l, b: (l, b, 0, 0)),
            pl.BlockSpec((None, None, MEM_LEN, M_WIDTH), lambda l, b: (l, b, 0, 0)),
        ],
        out_shape=[out, out],
        compiler_params=pltpu.CompilerParams(
            dimension_semantics=("arbitrary", "arbitrary"),
            vmem_limit_bytes=VMEM_LIMIT_BYTES),
        name="mem_kv",
    )(mem, mem_norm_g.reshape(DEPTH, 1, D_MODEL), w_mem_kv)


def _qkv_body(x, g_ref, wqvT_ref, wk_ref, gqT_ref, gk_ref, cosT_ref, sinT_ref,
              cos_ref, sin_ref, ones_ref, qT_ref, k_ref, vT_ref):
    tm = x.shape[0]
    h = _rms(x, g_ref[...]).astype(BF16)
    qvT = lax.dot_general(wqvT_ref[...], h, (((1,), (1,)), ((), ())),
                          preferred_element_type=F32)
    gq = jnp.tile(gqT_ref[...], (1, tm // LANES))
    cosT = cosT_ref[...]
    sinT = sinT_ref[...]
    zeros = jnp.zeros((A_HEAD_DIM, tm), BF16)
    half_w = A_HEAD_DIM // 4
    for hd in range(A_HEADS):
        t = qvT[hd * A_HEAD_DIM:(hd + 1) * A_HEAD_DIM, :]
        ms = jnp.mean(t * t, axis=0, keepdims=True)
        tn = t * lax.rsqrt(ms + EPS) * gq
        blocks = [tn[r:r + half_w, :] for r in range(0, A_HEAD_DIM, half_w)]
        partner = jnp.concatenate([blocks[i ^ 1] for i in range(len(blocks))], axis=0)
        piece = (tn * cosT + partner * sinT).astype(BF16)
        lo_rows = pl.ds(hd * LANES, A_HEAD_DIM)
        hi_rows = pl.ds(hd * LANES + A_HEAD_DIM, A_HEAD_DIM)
        if hd // (A_HEADS // A_KV_HEADS) == 0:
            qT_ref[lo_rows, :] = piece
            qT_ref[hi_rows, :] = zeros
        else:
            qT_ref[lo_rows, :] = zeros
            qT_ref[hi_rows, :] = piece
    vT_ref[...] = qvT[A_WIDTH:, :].astype(BF16)

    kk = _dot(h, wk_ref[...])
    ms = _dot((kk * kk).astype(BF16), ones_ref[...]) * (1.0 / A_HEAD_DIM)
    kn = kk * lax.rsqrt(ms + EPS) * gk_ref[...]
    lane = lax.broadcasted_iota(jnp.int32, (tm, LANES), 1)
    partner = jnp.where((lane & half_w) == 0,
                        pltpu.roll(kn, LANES - half_w, 1),
                        pltpu.roll(kn, half_w, 1))
    k_ref[...] = (kn * cos_ref[...] + partner * sin_ref[...]).astype(BF16)


N_QKV_PARAMS = 10


def _qkv_kernel(x_ref, *refs):
    _qkv_body(x_ref[...], *refs)


def _qkv_in_specs(tm, layer):
    return [
        _layer_spec((1, D_MODEL), layer),
        _layer_spec((A_WIDTH + A_KV_WIDTH, D_MODEL), layer),
        _layer_spec((D_MODEL, A_KV_WIDTH), layer),
        _layer_spec((A_HEAD_DIM, LANES), layer),
        _layer_spec((1, LANES), layer),
        pl.BlockSpec((A_HEAD_DIM, tm), lambda b, i: (0, i)),
        pl.BlockSpec((A_HEAD_DIM, tm), lambda b, i: (0, i)),
        pl.BlockSpec((tm, LANES), lambda b, i: (i, 0)),
        pl.BlockSpec((tm, LANES), lambda b, i: (i, 0)),
        _const_spec((LANES, LANES)),
    ]


def _qkv_out_specs(tm):
    return [
        pl.BlockSpec((None, A_HEADS * LANES, tm), lambda b, i: (b, 0, i)),
        pl.BlockSpec((None, tm, A_KV_WIDTH), lambda b, i: (b, i, 0)),
        pl.BlockSpec((None, A_KV_WIDTH, tm), lambda b, i: (b, 0, i)),
    ]


def _qkv_out_shapes(bsz, seq):
    return [
        jax.ShapeDtypeStruct((bsz, A_HEADS * LANES, seq), BF16),
        jax.ShapeDtypeStruct((bsz, seq, A_KV_WIDTH), BF16),
        jax.ShapeDtypeStruct((bsz, A_KV_WIDTH, seq), BF16),
    ]


def _qkv(x, layer, qkv_params):
    bsz, seq, _ = x.shape
    tm = QKV_TILE
    return pl.pallas_call(
        _qkv_kernel,
        grid=(bsz, seq // tm),
        in_specs=[pl.BlockSpec((None, tm, D_MODEL), lambda b, i: (b, i, 0))]
        + _qkv_in_specs(tm, layer),
        out_specs=_qkv_out_specs(tm),
        out_shape=_qkv_out_shapes(bsz, seq),
        compiler_params=pltpu.CompilerParams(
            dimension_semantics=("arbitrary", "arbitrary"),
            vmem_limit_bytes=VMEM_LIMIT_BYTES),
        name="qkv",
    )(x, *qkv_params)


def _attn_kernel(q_ref, k_ref, vT_ref, qn_ref, kn_ref, o_ref, s_even, s_odd, m_ref, o_pair):
    seq, tq = s_even.shape
    n_chunks = seq // ATTN_KV_CHUNK
    grp = A_HEADS // A_KV_HEADS

    def rows8(a):
        return a.reshape(a.shape[0] // 8, 8, tq)

    def stage(q, k_src, s_qk, t_sm, s_sm, m_sm, odd=False, flush=None):
        m8 = None
        if flush is not None:
            o_ref[flush] = o_pair[...].T
        if t_sm is not None:
            v_rows = pl.ds(pl.multiple_of((t_sm // grp) * A_HEAD_DIM, A_HEAD_DIM), A_HEAD_DIM)
            acc = jnp.zeros((A_HEAD_DIM, tq), F32)
            l8 = jnp.zeros((8, tq), F32)
        for c in range(n_chunks):
            kv = slice(c * ATTN_KV_CHUNK, (c + 1) * ATTN_KV_CHUNK)
            sc = _dot(k_src[kv, :], q)
            s_qk[kv, :] = sc
            cm = jnp.max(rows8(sc), axis=0)
            m8 = cm if m8 is None else jnp.maximum(m8, cm)
            if t_sm is not None:
                p = jnp.exp2(s_sm[kv, :] - m_sm)
                l8 = l8 + jnp.sum(rows8(p), axis=0)
                acc = acc + _dot(vT_ref[v_rows, kv], p.astype(BF16))
        if t_sm is not None:
            o = acc * (1.0 / jnp.sum(l8, axis=0, keepdims=True))
            o_pair[pl.ds(A_HEAD_DIM if odd else 0, A_HEAD_DIM), :] = o
        return jnp.max(m8, axis=0, keepdims=True)

    @pl.when((pl.program_id(0) == 0) & (pl.program_id(1) == 0))
    def _():
        m_ref[...] = stage(q_ref[0], k_ref, s_even, None, None, None)

    def pair(u, m_odd):
        m_even = stage(q_ref[2 * u + 2], k_ref, s_even, 2 * u + 1, s_odd, m_odd, odd=True)
        return stage(q_ref[2 * u + 3], k_ref, s_odd, 2 * u + 2, s_even, m_even, flush=u)

    m_odd = stage(q_ref[1], k_ref, s_odd, 0, s_even, m_ref[...])
    m_odd = lax.fori_loop(0, A_HEADS // 2 - 1, pair, m_odd)
    m_ref[...] = stage(qn_ref[...], kn_ref, s_even, A_HEADS - 1, s_odd, m_odd, odd=True)
    o_ref[A_HEADS // 2 - 1] = o_pair[...].T


def _attn(qT, k, vT):
    bsz, seq, _ = k.shape
    tq = ATTN_TILE
    n_i = seq // tq

    def next_step(b, i):
        nxt = jnp.minimum(b * n_i + i + 1, bsz * n_i - 1)
        return nxt // n_i, nxt % n_i

    def qn_map(b, i):
        nb, ni = next_step(b, i)
        return nb, 0, 0, ni

    def kn_map(b, i):
        return next_step(b, i)[0], 0, 0

    q4 = qT.reshape(bsz, A_HEADS, LANES, seq)
    return pl.pallas_call(
        _attn_kernel,
        grid=(bsz, n_i),
        in_specs=[
            pl.BlockSpec((None, A_HEADS, LANES, tq), lambda b, i: (b, 0, 0, i)),
            pl.BlockSpec((None, seq, A_KV_WIDTH), lambda b, i: (b, 0, 0)),
            pl.BlockSpec((None, A_KV_WIDTH, seq), lambda b, i: (b, 0, 0)),
            pl.BlockSpec((None, None, LANES, tq), qn_map),
            pl.BlockSpec((None, seq, A_KV_WIDTH), kn_map),
        ],
        out_specs=pl.BlockSpec((None, A_WIDTH // LANES, tq, LANES), lambda b, i: (b, 0, i, 0)),
        out_shape=jax.ShapeDtypeStruct((bsz, A_WIDTH // LANES, seq, LANES), F32),
        scratch_shapes=[pltpu.VMEM((seq, tq), F32), pltpu.VMEM((seq, tq), F32),
                        pltpu.VMEM((1, tq), F32), pltpu.VMEM((2 * A_HEAD_DIM, tq), F32)],
        compiler_params=pltpu.CompilerParams(
            dimension_semantics=("arbitrary", "arbitrary"),
            vmem_limit_bytes=VMEM_LIMIT_BYTES),
        name="attn",
    )(q4, k, vT, q4, k)


def _rest_kernel(x_ref, yA_ref, kmem_ref, vmem_ref, g_ref, w_ref, lng_ref, lnb_ref,
                 ws_ref, bs_ref, wbr_ref, wout_ref, *refs, final):
    if final:
        fg_ref, o_ref = refs
    else:
        qkv_in, o_ref, qkv_out = refs[:N_QKV_PARAMS], refs[N_QKV_PARAMS], refs[N_QKV_PARAMS + 1:]
    tq = x_ref.shape[0]
    x = x_ref[...]
    h = _rms(x, g_ref[...]).astype(BF16)
    proj = _dot(h, w_ref[:, QKV_WIDTH:])
    hzA = proj[:, COL_ZA:COL_UB]
    uB = proj[:, COL_UB:COL_VB]
    vB = proj[:, COL_VB:COL_ZB]
    hzB = proj[:, COL_ZB:COL_QM]
    qM = proj[:, COL_QM:COL_ZM]
    hzM = proj[:, COL_ZM:COL_GATE]

    yA = jnp.concatenate([yA_ref[j] for j in range(A_WIDTH // LANES)], axis=1) * _half_silu(hzA)

    mu = jnp.mean(vB, axis=-1, keepdims=True)
    vc = vB - mu
    vn = vc * lax.rsqrt(jnp.mean(vc * vc, axis=-1, keepdims=True) + EPS)
    vn = (vn * lng_ref[...] + lnb_ref[...]).astype(BF16)
    n_chunk = tq // CHUNK
    per_group = []
    for g in range(B_GROUPS):
        gcols = slice(g * B_GROUP_DIM, (g + 1) * B_GROUP_DIM)
        wide = jnp.concatenate([vn[c * CHUNK:(c + 1) * CHUNK, gcols] for c in range(n_chunk)],
                               axis=1)
        per_group.append(_dot(ws_ref[g], wide))
    bias = jnp.concatenate([bs_ref[g] for g in range(B_GROUPS)], axis=1)
    mixed = jnp.concatenate(
        [jnp.concatenate([per_group[g][:, c * B_GROUP_DIM:(c + 1) * B_GROUP_DIM]
                          for g in range(B_GROUPS)], axis=1) + bias
         for c in range(n_chunk)], axis=0)
    yB = uB * mixed * _half_silu(hzB)

    heads = []
    for hh in range(M_HEADS):
        sl = slice(hh * M_HEAD_DIM, (hh + 1) * M_HEAD_DIM)
        s = lax.dot_general(qM[:, sl].astype(BF16), kmem_ref[:, sl],
                            (((1,), (1,)), ((), ())), preferred_element_type=F32)
        p = jnp.exp2(s - jnp.max(s, axis=-1, keepdims=True))
        l = jnp.sum(p, axis=-1, keepdims=True)
        heads.append(_dot(p.astype(BF16), vmem_ref[:, sl]) * (1.0 / l))
    yM = jnp.concatenate(heads, axis=1) * _half_silu(hzM)

    merged2 = None
    for n, y in enumerate((yA, yB, yM)):
        up = _dot(y.astype(BF16), wbr_ref[n])
        gate2 = 1.0 + jnp.tanh(proj[:, COL_GATE + n * D_MODEL:COL_GATE + (n + 1) * D_MODEL])
        merged2 = gate2 * up if merged2 is None else merged2 + gate2 * up
    out = x + _dot(merged2.astype(BF16), wout_ref[...])
    if final:
        o_ref[...] = _rms(out, fg_ref[...])
    else:
        o_ref[...] = out
        _qkv_body(out, *qkv_in, *qkv_out)


def _rest(x, yAT, kmem, vmem, layer, rest_params, tail_params, final):
    bsz, seq, _ = x.shape
    tq = REST_TILE
    x_spec = pl.BlockSpec((None, tq, D_MODEL), lambda b, i: (b, i, 0))
    x_shape = jax.ShapeDtypeStruct((bsz, seq, D_MODEL), F32)
    if final:
        tail_specs = [_const_spec((1, D_MODEL))]
        out_specs, out_shape = x_spec, x_shape
    else:
        tail_specs = _qkv_in_specs(tq, layer + 1)
        out_specs = [x_spec] + _qkv_out_specs(tq)
        out_shape = [x_shape] + _qkv_out_shapes(bsz, seq)
    return pl.pallas_call(
        functools.partial(_rest_kernel, final=final),
        grid=(bsz, seq // tq),
        in_specs=[
            pl.BlockSpec((None, tq, D_MODEL), lambda b, i: (b, i, 0)),
            pl.BlockSpec((None, A_WIDTH // LANES, tq, LANES), lambda b, i: (b, 0, i, 0)),
            pl.BlockSpec((None, None, MEM_LEN, M_WIDTH), lambda b, i: (layer, b, 0, 0)),
            pl.BlockSpec((None, None, MEM_LEN, M_WIDTH), lambda b, i: (layer, b, 0, 0)),
            _layer_spec((1, D_MODEL), layer),
            _layer_spec((D_MODEL, IN_WIDTH), layer),
            _layer_spec((1, B_WIDTH), layer),
            _layer_spec((1, B_WIDTH), layer),
            _layer_spec((B_GROUPS, CHUNK, CHUNK), layer),
            _layer_spec((B_GROUPS, CHUNK, B_GROUP_DIM), layer),
            _layer_spec((N_BRANCH, BRANCH_WIDTH, D_MODEL), layer),
            _layer_spec((D_MODEL, D_MODEL), layer),
        ] + tail_specs,
        out_specs=out_specs,
        out_shape=out_shape,
        compiler_params=pltpu.CompilerParams(
            dimension_semantics=("arbitrary", "arbitrary"),
            vmem_limit_bytes=VMEM_LIMIT_BYTES),
        name="rest",
    )(x, yAT, kmem, vmem, *rest_params, *tail_params)


def _rope_tables(seq):
    rows = seq // GRID_W
    row = jnp.repeat(jnp.arange(rows, dtype=F32), GRID_W)
    col = jnp.tile(jnp.arange(GRID_W, dtype=F32), rows)
    n_freq = A_HEAD_DIM // 4
    inv = ROPE_THETA ** (-jnp.arange(n_freq, dtype=F32) / n_freq)
    ang = jnp.stack([row[:, None] * inv, col[:, None] * inv], axis=1)
    cos, sin = jnp.cos(ang), jnp.sin(ang)
    cos_hd = jnp.broadcast_to(cos[:, :, None, :], (seq, 2, 2, n_freq)).reshape(seq, A_HEAD_DIM)
    sin_hd = jnp.stack([-sin, sin], axis=2).reshape(seq, A_HEAD_DIM)
    reps = LANES // A_HEAD_DIM
    return jnp.tile(cos_hd, (1, reps)), jnp.tile(sin_hd, (1, reps)), cos_hd.T, sin_hd.T


def kernel(x, mem, norm_g, w_in, q_norm_g, k_norm_g, sg_ln_g, sg_ln_b, w_s, b_s,
           mem_norm_g, w_mem_kv, w_br, w_out, final_g):
    bsz, seq, _ = x.shape
    cos, sin, cosT, sinT = _rope_tables(seq)
    lane_head = jnp.arange(LANES) // A_HEAD_DIM
    ones = (lane_head[:, None] == lane_head[None, :]).astype(BF16)
    reps = LANES // A_HEAD_DIM
    q_scale = A_HEAD_DIM ** -0.5 * LOG2E

    col = jnp.arange(IN_WIDTH) - QKV_WIDTH
    halved = (((col >= COL_ZA) & (col < COL_UB)) | ((col >= COL_ZB) & (col < COL_QM))
              | (col >= COL_ZM))
    w_in_bf = (w_in * jnp.where(halved, 0.5, 1.0).astype(F32)).astype(BF16)
    g = norm_g.reshape(DEPTH, 1, D_MODEL)
    gqT = jnp.broadcast_to((q_norm_g * q_scale)[:, :, None], (DEPTH, A_HEAD_DIM, LANES))
    w_qvT = jnp.concatenate([w_in_bf[:, :, :A_WIDTH],
                             w_in_bf[:, :, A_WIDTH + A_KV_WIDTH:QKV_WIDTH]], axis=2).transpose(0, 2, 1)
    w_k = w_in_bf[:, :, A_WIDTH:A_WIDTH + A_KV_WIDTH]
    gk = jnp.tile(k_norm_g, (1, reps)).reshape(DEPTH, 1, LANES)
    lng = sg_ln_g.reshape(DEPTH, 1, B_WIDTH)
    lnb = sg_ln_b.reshape(DEPTH, 1, B_WIDTH)
    ws = w_s.astype(BF16)
    bs = jnp.broadcast_to(b_s[:, :, :, None], (DEPTH, B_GROUPS, CHUNK, B_GROUP_DIM))
    wbr = w_br.astype(BF16)
    wout = (w_out * 0.5).astype(BF16)
    fg = final_g.reshape(1, D_MODEL)
    kmem, vmem = _mem_kv(mem, mem_norm_g, w_mem_kv.astype(BF16))
    qkv_params = (g, w_qvT, w_k, gqT, gk, cosT, sinT, cos, sin, ones)
    assert len(qkv_params) == N_QKV_PARAMS
    rest_params = (g, w_in_bf, lng, lnb, ws, bs, wbr, wout)
    qT, k, vT = _qkv(x, 0, qkv_params)
    for l in range(DEPTH - 1):
        yAT = _attn(qT, k, vT)
        x, qT, k, vT = _rest(x, yAT, kmem, vmem, l, rest_params, qkv_params, final=False)
    yAT = _attn(qT, k, vT)
    return _rest(x, yAT, kmem, vmem, DEPTH - 1, rest_params, (fg,), final=True)
```
